```python
import math
import jax, jax.numpy as jnp
from jax import lax
import numpy as np

D_MODEL = 1024
BATCH = 16
SEQ = 256
DEPTH = 2
DEC_BATCH = 4
DEC_SEQ = 4096
PAST_LEN = 256

GRID_W = 64
CHUNK = 128
CONV_W = 5
N_DIR = 2
N_ML = (DEPTH + 1) // 2
N_SS = DEPTH // 2
M_INNER = 2 * D_MODEL
M_HEADS = 4
M_DH = M_INNER // M_HEADS
S_INNER = 2 * D_MODEL
S_HEADDIM = 64
S_HEADS = S_INNER // S_HEADDIM
S_GROUPS = 8
S_HPG = S_HEADS // S_GROUPS
S_STATE = 128
S_GN = S_GROUPS * S_STATE
S_CONV_CH = S_INNER + 2 * S_GN
S_IN_COLS = S_INNER + S_CONV_CH + N_DIR * S_HEADS
D_FF = 2816
N_EXPERTS = 8
TOP_K = 2
E_FF = 3584
ALPHA = (2 * DEPTH) ** 0.25
BETA = (8 * DEPTH) ** -0.25
LN_EPS = 1e-5
RMS_EPS = 1e-5
DT_MIN = 0.001
DT_MAX = 0.1
POS_BASE = 10000.0

kernel_name = 'hybrid_mlstm_mamba2_flow_trunk_step'


def layer_norm(x, g, b):
    xf = x.astype(jnp.float32)
    mu = jnp.mean(xf, axis=-1, keepdims=True)
    var = jnp.mean(jnp.square(xf - mu), axis=-1, keepdims=True)
    y = (xf - mu) * lax.rsqrt(var + LN_EPS) * g.astype(jnp.float32) + b.astype(jnp.float32)
    return y.astype(x.dtype)


def centred_conv(x, w, b):
    pad = CONV_W // 2
    y = lax.conv_general_dilated(x.astype(jnp.float32), w.astype(jnp.float32)[:, None, :],
                                 window_strides=(1,), padding=[(pad, pad)],
                                 dimension_numbers=('NWC', 'WIO', 'NWC'),
                                 feature_group_count=x.shape[-1])
    return y + b.astype(jnp.float32)


def grid_pos_embed(n_tokens):
    rows = n_tokens // GRID_W
    quarter = D_MODEL // 4
    freqs = jnp.exp(-math.log(POS_BASE) * jnp.arange(quarter, dtype=jnp.float32) / quarter)
    er = jnp.arange(rows, dtype=jnp.float32)[:, None] * freqs
    ec = jnp.arange(GRID_W, dtype=jnp.float32)[:, None] * freqs
    row_emb = jnp.concatenate([jnp.sin(er), jnp.cos(er)], axis=-1)
    col_emb = jnp.concatenate([jnp.sin(ec), jnp.cos(ec)], axis=-1)
    pos = jnp.concatenate([jnp.broadcast_to(row_emb[:, None, :], (rows, GRID_W, D_MODEL // 2)),
                           jnp.broadcast_to(col_emb[None, :, :], (rows, GRID_W, D_MODEL // 2))], axis=-1)
    return pos.reshape(rows * GRID_W, D_MODEL)


def mlstm_scan(q, k, v, log_i, log_f, c0, n0, m0):
    bsz, nh, seqlen, dh = q.shape
    nc = seqlen // CHUNK

    def to_chunks(a):
        a = a.reshape(bsz, nh, nc, CHUNK, *a.shape[3:])
        return jnp.moveaxis(a, 2, 0)

    causal = jnp.tril(jnp.ones((CHUNK, CHUNK), dtype=bool))

    def step(carry, xs):
        c, n, m = carry
        qc, kc, vc, lic, lfc = xs
        b = jnp.cumsum(lfc, axis=-1)
        d = b[..., :, None] - b[..., None, :] + lic[..., None, :]
        d = jnp.where(causal, d, -jnp.inf)
        g = b + m[..., None]
        m_t = jnp.maximum(g, jnp.max(d, axis=-1))
        s = jnp.einsum('bhtd,bhsd->bhts', qc, kc) * jnp.exp(d - m_t[..., None])
        inter = jnp.exp(g - m_t)
        num = jnp.einsum('bhts,bhse->bhte', s, vc) + inter[..., None] * jnp.einsum('bhtd,bhde->bhte', qc, c)
        den = jnp.sum(s, axis=-1) + inter * jnp.einsum('bhtd,bhd->bht', qc, n)
        h = num / jnp.maximum(jnp.abs(den), jnp.exp(-m_t))[..., None]
        b_last = b[..., -1]
        ld = b_last[..., None] - b + lic
        m_new = jnp.maximum(b_last + m, jnp.max(ld, axis=-1))
        wk = jnp.exp(ld - m_new[..., None])
        decay = jnp.exp(b_last + m - m_new)
        c_new = decay[..., None, None] * c + jnp.einsum('bhsd,bhse->bhde', kc * wk[..., None], vc)
        n_new = decay[..., None] * n + jnp.einsum('bhs,bhsd->bhd', wk, kc)
        return (c_new, n_new, m_new), h

    (c, n, m), h = lax.scan(step, (c0, n0, m0), tuple(to_chunks(a) for a in (q, k, v, log_i, log_f)))
    h = jnp.moveaxis(h, 0, 2).reshape(bsz, nh, seqlen, dh)
    return h, c, n, m


def mlstm_mixer(h, w_up, conv_w, conv_b, wq, wk, wv, w_ig, b_ig, w_fg, b_fg, norm_w, skip, w_down, c0, n0, m0):
    bsz, seqlen, _ = h.shape
    xm, z = jnp.split((h @ w_up).astype(jnp.float32), 2, axis=-1)
    xc = jax.nn.silu(centred_conv(xm, conv_w, conv_b))
    xc_h = xc.reshape(bsz, seqlen, M_HEADS, M_DH)
    q = jnp.einsum('bshd,hde->bshe', xc_h, wq.astype(jnp.float32))
    k = jnp.einsum('bshd,hde->bshe', xc_h, wk.astype(jnp.float32)) * (M_DH ** -0.5)
    v = jnp.einsum('bshd,hde->bshe', xm.reshape(bsz, seqlen, M_HEADS, M_DH), wv.astype(jnp.float32))
    qkv = jnp.concatenate([q, k, v], axis=2).reshape(bsz, seqlen, 3 * M_INNER)
    i_pre = jnp.einsum('bsc,rch->rbhs', qkv, w_ig.astype(jnp.float32)) + b_ig.astype(jnp.float32)[:, None, :, None]
    f_pre = jnp.einsum('bsc,rch->rbhs', qkv, w_fg.astype(jnp.float32)) + b_fg.astype(jnp.float32)[:, None, :, None]
    log_f = jax.nn.log_sigmoid(f_pre)
    q, k, v = (jnp.transpose(t, (0, 2, 1, 3)) for t in (q, k, v))
    c0 = c0.astype(jnp.float32)
    n0 = n0.astype(jnp.float32)
    m0 = m0.astype(jnp.float32)
    hsum = jnp.zeros((bsz, M_HEADS, seqlen, M_DH), jnp.float32)
    cs, ns, ms = [], [], []
    for r in range(N_DIR):
        flip = (lambda t: jnp.flip(t, axis=2)) if r == 1 else (lambda t: t)
        h_r, c_r, n_r, m_r = mlstm_scan(flip(q), flip(k), flip(v), flip(i_pre[r]), flip(log_f[r]),
                                        c0[:, r], n0[:, r], m0[:, r])
        hsum = hsum + flip(h_r)
        cs.append(c_r)
        ns.append(n_r)
        ms.append(m_r)
    mu = jnp.mean(hsum, axis=-1, keepdims=True)
    var = jnp.mean(jnp.square(hsum - mu), axis=-1, keepdims=True)
    hn = (hsum - mu) * lax.rsqrt(var + LN_EPS)
    hn = jnp.transpose(hn, (0, 2, 1, 3)).reshape(bsz, seqlen, M_INNER) * norm_w.astype(jnp.float32)
    out = (hn + skip.astype(jnp.float32) * xc) * jax.nn.silu(z)
    return out.astype(h.dtype) @ w_down, jnp.stack(cs, axis=1), jnp.stack(ns, axis=1), jnp.stack(ms, axis=1)


def ssd_scan(x, dt, a, bm, cm, s0):
    bsz, seqlen = x.shape[:2]
    nc = seqlen // CHUNK

    def to_chunks(t):
        return jnp.moveaxis(t.reshape(bsz, nc, CHUNK, *t.shape[2:]), 1, 0)

    causal = jnp.tril(jnp.ones((CHUNK, CHUNK), dtype=bool))[None, :, :, None, None]

    def step(state, xs):
        xc, dtc, bc, cc = xs
        cum = jnp.cumsum(dtc * a, axis=1)
        seg = cum[:, :, None] - cum[:, None, :]
        decay = jnp.exp(jnp.where(causal, seg, -jnp.inf))
        cb = jnp.einsum('btgn,bsgn->btsg', cc, bc)[..., None]
        xdt = xc * dtc[..., None]
        y = jnp.einsum('btsgj,bsgjp->btgjp', decay * cb, xdt)
        y = y + jnp.exp(cum)[..., None] * jnp.einsum('btgn,bgjpn->btgjp', cc, state)
        last = cum[:, -1]
        wts = jnp.exp(last[:, None] - cum) * dtc
        state = jnp.exp(last)[..., None, None] * state + jnp.einsum('bsgj,bsgjp,bsgn->bgjpn', wts, xc, bc)
        return state, y

    s_fin, y = lax.scan(step, s0, tuple(to_chunks(t) for t in (x, dt, bm, cm)))
    y = jnp.moveaxis(y, 0, 1).reshape(x.shape)
    return y, s_fin


def mamba_mixer(h, w_in, conv_w, conv_b, dt_bias, a_log, d_skip, norm_w, w_out, s0):
    bsz, seqlen, _ = h.shape
    proj = (h @ w_in).astype(jnp.float32)
    z = proj[..., :S_INNER]
    xbc = jax.nn.silu(centred_conv(proj[..., S_INNER:S_INNER + S_CONV_CH], conv_w, conv_b))
    dt_raw = proj[..., S_INNER + S_CONV_CH:].reshape(bsz, seqlen, N_DIR, S_HEADS)
    xs = xbc[..., :S_INNER].reshape(bsz, seqlen, S_GROUPS, S_HPG, S_HEADDIM)
    bm = xbc[..., S_INNER:S_INNER + S_GN].reshape(bsz, seqlen, S_GROUPS, S_STATE)
    cm = xbc[..., S_INNER + S_GN:].reshape(bsz, seqlen, S_GROUPS, S_STATE)
    y = d_skip.astype(jnp.float32).reshape(S_GROUPS, S_HPG)[:, :, None] * xs
    s0 = s0.astype(jnp.float32)
    finals = []
    for r in range(N_DIR):
        flip = (lambda t: jnp.flip(t, axis=1)) if r == 1 else (lambda t: t)
        dt = jax.nn.softplus(dt_raw[:, :, r] + dt_bias[r].astype(jnp.float32)).reshape(bsz, seqlen, S_GROUPS, S_HPG)
        a = -jnp.exp(a_log[r].astype(jnp.float32)).reshape(S_GROUPS, S_HPG)
        y_r, s_r = ssd_scan(flip(xs), flip(dt), a, flip(bm), flip(cm),
                            s0[:, r].reshape(bsz, S_GROUPS, S_HPG, S_HEADDIM, S_STATE))
        y = y + flip(y_r)
        finals.append(s_r.reshape(bsz, S_HEADS, S_HEADDIM, S_STATE))
    y = y.reshape(bsz, seqlen, S_INNER) * jax.nn.silu(z)
    y = y * lax.rsqrt(jnp.mean(jnp.square(y), axis=-1, keepdims=True) + RMS_EPS) * norm_w.astype(jnp.float32)
    return y.astype(h.dtype) @ w_out, jnp.stack(finals, axis=1)


def swiglu(h, w1, w3, w2):
    return (jax.nn.silu(h @ w1) * (h @ w3)) @ w2


def moe_swiglu(h, w_router, w1, w3, w2):
    logits = (h @ w_router).astype(jnp.float32)
    top_v, top_i = lax.top_k(logits, TOP_K)
    top_w = jax.nn.softmax(top_v, axis=-1)
    gates = jnp.sum(jax.nn.one_hot(top_i, N_EXPERTS, dtype=jnp.float32) * top_w[..., None], axis=-2)
    out = jnp.zeros(h.shape, jnp.float32)
    for e in range(N_EXPERTS):
        out = out + gates[..., e:e + 1] * swiglu(h, w1[e], w3[e], w2[e])
    return out.astype(h.dtype)


def run_trunk(x, cond, st_c, st_n, st_m, st_s, p):
    new_c, new_n, new_m, new_s = [], [], [], []
    for i in range(DEPTH):
        j = i // 2
        mod = (jax.nn.silu(cond) @ p['ada_w'][i] + p['ada_b'][i])[:, None, :]
        sh1, sc1, g1, sh2, sc2, g2 = jnp.split(mod, 6, axis=-1)
        h = x * (1 + sc1) + sh1
        if i % 2 == 0:
            out, c_f, n_f, m_f = mlstm_mixer(h, p['ml_w_up'][j], p['ml_conv_w'][j], p['ml_conv_b'][j],
                                             p['ml_wq'][j], p['ml_wk'][j], p['ml_wv'][j],
                                             p['ml_w_ig'][j], p['ml_b_ig'][j], p['ml_w_fg'][j], p['ml_b_fg'][j],
                                             p['ml_norm_w'][j], p['ml_skip'][j], p['ml_w_down'][j],
                                             st_c[:, j], st_n[:, j], st_m[:, j])
            new_c.append(c_f)
            new_n.append(n_f)
            new_m.append(m_f)
        else:
            out, s_f = mamba_mixer(h, p['ss_w_in'][j], p['ss_conv_w'][j], p['ss_conv_b'][j],
                                   p['ss_dt_bias'][j], p['ss_a_log'][j], p['ss_d'][j],
                                   p['ss_norm_w'][j], p['ss_w_out'][j], st_s[:, j])
            new_s.append(s_f)
        x = layer_norm(ALPHA * x + g1 * out, p['ln_g'][i, 0], p['ln_b'][i, 0])
        h = x * (1 + sc2) + sh2
        if i % 2 == 0:
            f = swiglu(h, p['ff_w1'][j], p['ff_w3'][j], p['ff_w2'][j])
        else:
            f = moe_swiglu(h, p['moe_w_router'][j], p['moe_w1'][j], p['moe_w3'][j], p['moe_w2'][j])
        x = layer_norm(ALPHA * x + g2 * f, p['ln_g'][i, 1], p['ln_b'][i, 1])
    return x, jnp.stack(new_c, axis=1), jnp.stack(new_n, axis=1), jnp.stack(new_m, axis=1), jnp.stack(new_s, axis=1)


def setup_inputs(seed: int = 0) -> dict:
    key = jax.random.key(seed)
    ks = iter(jax.random.split(key, 64))
    f32 = jnp.float32

    def nrm(shape, scale):
        return jax.random.normal(next(ks), shape, f32) * scale

    def unif(shape, lo, hi):
        return jax.random.uniform(next(ks), shape, f32, lo, hi)

    dt = jnp.exp(unif((N_SS, N_DIR, S_HEADS), math.log(DT_MIN), math.log(DT_MAX)))
    inp = {}
    inp['x_prompt'] = nrm((BATCH, SEQ, D_MODEL), 1.0)
    inp['x_sample'] = nrm((DEC_BATCH, DEC_SEQ, D_MODEL), 1.0)
    inp['c'] = nrm((DEC_BATCH, D_MODEL), 1.0)
    inp['state_mlstm_C'] = nrm((DEC_BATCH, N_ML, N_DIR, M_HEADS, M_DH, M_DH), 0.1)
    inp['state_mlstm_n'] = nrm((DEC_BATCH, N_ML, N_DIR, M_HEADS, M_DH), 0.1)
    inp['state_mlstm_m'] = unif((DEC_BATCH, N_ML, N_DIR, M_HEADS), 0.0, 3.0)
    inp['state_ssm'] = nrm((DEC_BATCH, N_SS, N_DIR, S_HEADS, S_HEADDIM, S_STATE), 0.1)
    inp['c_ctx'] = nrm((D_MODEL,), 1.0)
    inp['ada_w'] = nrm((DEPTH, D_MODEL, 6 * D_MODEL), 0.5 * D_MODEL ** -0.5)
    inp['ada_b'] = nrm((DEPTH, 6 * D_MODEL), 0.02)
    inp['ln_g'] = 1.0 + nrm((DEPTH, 2, D_MODEL), 0.02)
    inp['ln_b'] = nrm((DEPTH, 2, D_MODEL), 0.02)
    inp['ml_w_up'] = nrm((N_ML, D_MODEL, 2 * M_INNER), D_MODEL ** -0.5)
    inp['ml_conv_w'] = nrm((N_ML, CONV_W, M_INNER), CONV_W ** -0.5)
    inp['ml_conv_b'] = nrm((N_ML, M_INNER), 0.02)
    inp['ml_wq'] = nrm((N_ML, M_HEADS, M_DH, M_DH), M_DH ** -0.5)
    inp['ml_wk'] = nrm((N_ML, M_HEADS, M_DH, M_DH), M_DH ** -0.5)
    inp['ml_wv'] = nrm((N_ML, M_HEADS, M_DH, M_DH), M_DH ** -0.5)
    inp['ml_w_ig'] = nrm((N_ML, N_DIR, 3 * M_INNER, M_HEADS), (3 * M_INNER) ** -0.5)
    inp['ml_b_ig'] = nrm((N_ML, N_DIR, M_HEADS), 0.1)
    inp['ml_w_fg'] = nrm((N_ML, N_DIR, 3 * M_INNER, M_HEADS), (3 * M_INNER) ** -0.5)
    inp['ml_b_fg'] = unif((N_ML, N_DIR, M_HEADS), 3.0, 6.0)
    inp['ml_norm_w'] = 1.0 + nrm((N_ML, M_INNER), 0.02)
    inp['ml_skip'] = 1.0 + nrm((N_ML, M_INNER), 0.02)
    inp['ml_w_down'] = nrm((N_ML, M_INNER, D_MODEL), M_INNER ** -0.5 * BETA)
    inp['ss_w_in'] = nrm((N_SS, D_MODEL, S_IN_COLS), D_MODEL ** -0.5)
    inp['ss_conv_w'] = nrm((N_SS, CONV_W, S_CONV_CH), CONV_W ** -0.5)
    inp['ss_conv_b'] = nrm((N_SS, S_CONV_CH), 0.02)
    inp['ss_dt_bias'] = dt + jnp.log(-jnp.expm1(-dt))
    inp['ss_a_log'] = jnp.log(unif((N_SS, N_DIR, S_HEADS), 1.0, 16.0))
    inp['ss_d'] = 1.0 + nrm((N_SS, S_HEADS), 0.02)
    inp['ss_norm_w'] = 1.0 + nrm((N_SS, S_INNER), 0.02)
    inp['ss_w_out'] = nrm((N_SS, S_INNER, D_MODEL), S_INNER ** -0.5 * BETA)
    inp['ff_w1'] = nrm((N_ML, D_MODEL, D_FF), D_MODEL ** -0.5)
    inp['ff_w3'] = nrm((N_ML, D_MODEL, D_FF), D_MODEL ** -0.5)
    inp['ff_w2'] = nrm((N_ML, D_FF, D_MODEL), D_FF ** -0.5 * BETA)
    inp['moe_w_router'] = nrm((N_SS, D_MODEL, N_EXPERTS), D_MODEL ** -0.5)
    inp['moe_w1'] = nrm((N_SS, N_EXPERTS, D_MODEL, E_FF), D_MODEL ** -0.5)
    inp['moe_w3'] = nrm((N_SS, N_EXPERTS, D_MODEL, E_FF), D_MODEL ** -0.5)
    inp['moe_w2'] = nrm((N_SS, N_EXPERTS, E_FF, D_MODEL), E_FF ** -0.5 * BETA)
    return inp


def reference(x_prompt, x_sample, c, state_mlstm_C, state_mlstm_n, state_mlstm_m, state_ssm, c_ctx,
              ada_w, ada_b, ln_g, ln_b,
              ml_w_up, ml_conv_w, ml_conv_b, ml_wq, ml_wk, ml_wv, ml_w_ig, ml_b_ig, ml_w_fg, ml_b_fg,
              ml_norm_w, ml_skip, ml_w_down,
              ss_w_in, ss_conv_w, ss_conv_b, ss_dt_bias, ss_a_log, ss_d, ss_norm_w, ss_w_out,
              ff_w1, ff_w3, ff_w2,
              moe_w_router, moe_w1, moe_w3, moe_w2):
    p = dict(ada_w=ada_w, ada_b=ada_b, ln_g=ln_g, ln_b=ln_b,
             ml_w_up=ml_w_up, ml_conv_w=ml_conv_w, ml_conv_b=ml_conv_b, ml_wq=ml_wq, ml_wk=ml_wk, ml_wv=ml_wv,
             ml_w_ig=ml_w_ig, ml_b_ig=ml_b_ig, ml_w_fg=ml_w_fg, ml_b_fg=ml_b_fg,
             ml_norm_w=ml_norm_w, ml_skip=ml_skip, ml_w_down=ml_w_down,
             ss_w_in=ss_w_in, ss_conv_w=ss_conv_w, ss_conv_b=ss_conv_b, ss_dt_bias=ss_dt_bias,
             ss_a_log=ss_a_log, ss_d=ss_d, ss_norm_w=ss_norm_w, ss_w_out=ss_w_out,
             ff_w1=ff_w1, ff_w3=ff_w3, ff_w2=ff_w2,
             moe_w_router=moe_w_router, moe_w1=moe_w1, moe_w3=moe_w3, moe_w2=moe_w2)
    bp = x_prompt.shape[0]
    zc = jnp.zeros((bp, N_ML, N_DIR, M_HEADS, M_DH, M_DH), jnp.float32)
    zn = jnp.zeros((bp, N_ML, N_DIR, M_HEADS, M_DH), jnp.float32)
    zm = jnp.zeros((bp, N_ML, N_DIR, M_HEADS), jnp.float32)
    zs = jnp.zeros((bp, N_SS, N_DIR, S_HEADS, S_HEADDIM, S_STATE), jnp.float32)
    y_prompt, new_c, new_n, new_m, new_s = run_trunk(x_prompt, c_ctx[None, :], zc, zn, zm, zs, p)
    x_lat = x_sample + grid_pos_embed(x_sample.shape[1]).astype(x_sample.dtype)
    y_sample, _, _, _, _ = run_trunk(x_lat, c, state_mlstm_C, state_mlstm_n, state_mlstm_m, state_ssm, p)
    return (y_prompt, y_sample, new_c, new_n, new_m, new_s)
```

```python
import functools
import math

import jax
import jax.numpy as jnp
from jax import lax
from jax.experimental import pallas as pl
from jax.experimental.pallas import tpu as pltpu

F32 = jnp.float32
BF16 = jnp.bfloat16

D_MODEL = 1024
BATCH = 16
SEQ = 256
DEC_BATCH = 4
DEC_SEQ = 4096
GRID_W = 64
CHUNK = 128
CONV_W = 5
N_DIR = 2
M_INNER = 2 * D_MODEL
M_HEADS = 4
M_DH = M_INNER // M_HEADS
S_INNER = 2 * D_MODEL
S_HEADDIM = 64
S_HEADS = S_INNER // S_HEADDIM
S_GROUPS = 8
S_HPG = S_HEADS // S_GROUPS
S_STATE = 128
S_GN = S_GROUPS * S_STATE
S_CONV_CH = S_INNER + 2 * S_GN
D_FF = 2816
N_EXPERTS = 8
E_FF = 3584
DEPTH = 2
ALPHA = (2 * DEPTH) ** 0.25
LN_EPS = 1e-5
RMS_EPS = 1e-5
POS_BASE = 10000.0

N_CTX = BATCH * SEQ
N_LAT = DEC_BATCH * DEC_SEQ
N_TOK = N_CTX + N_LAT
N_COND = 1 + DEC_BATCH
COND_ROWS = 8

TB = 256
NB = N_TOK // TB
NB_CTX = N_CTX // TB
NB_SEQ = DEC_SEQ // TB
HALO = 8
CPB = TB // CHUNK

TM_FF = 512
TF_FF = 1408
TF_MOE = 896
LANES = 128

VMEM_LIMIT = 56 * 1024 * 1024


def _cparams(sem):
    return pltpu.CompilerParams(dimension_semantics=sem, vmem_limit_bytes=VMEM_LIMIT)


def _cond_of_token(t0):
    return jnp.where(t0 < N_CTX, 0, 1 + (t0 - N_CTX) // DEC_SEQ)


def _lat_seq_of_block(s):
    return jnp.maximum(s - NB_CTX, 0) // NB_SEQ


def _bwd_block(s):
    k = s - NB_CTX
    return jnp.where(s < NB_CTX, s, NB_CTX + (k // NB_SEQ) * NB_SEQ + (NB_SEQ - 1 - k % NB_SEQ))


def _seq_start(s):
    return jnp.logical_or(s < NB_CTX, (s - NB_CTX) % NB_SEQ == 0)


def _seq_end(s):
    return jnp.logical_or(s < NB_CTX, (s - NB_CTX) % NB_SEQ == NB_SEQ - 1)


def _silu(x):
    return x * jax.nn.sigmoid(x)


def _bdot(a, b):
    return jnp.dot(a.astype(BF16), b.astype(BF16), preferred_element_type=F32)


def _tri_dot(tri, x):
    hi = x.astype(BF16)
    r1 = x - hi.astype(F32)
    mid = r1.astype(BF16)
    lo = (r1 - mid.astype(F32)).astype(BF16)
    d = lambda a: jnp.dot(tri, a, preferred_element_type=F32)
    return d(hi) + d(mid) + d(lo)


def _tri_masks():
    t = lax.broadcasted_iota(jnp.int32, (CHUNK, CHUNK), 0)
    u = lax.broadcasted_iota(jnp.int32, (CHUNK, CHUNK), 1)
    return u <= t, u >= t


def _layer_norm(r, g, b):
    mu = jnp.mean(r, axis=-1, keepdims=True)
    d = r - mu
    var = jnp.mean(d * d, axis=-1, keepdims=True)
    return d * lax.rsqrt(var + LN_EPS) * g + b


def _ada_kernel(cond_ref, w_ref, b_ref, o_ref):
    o_ref[...] = _bdot(_silu(cond_ref[...]), w_ref[...]) + b_ref[...]


def _ada_mod(cond, ada_w, ada_b):
    ncol = 6 * D_MODEL // D_MODEL
    out = pl.pallas_call(
        _ada_kernel,
        grid=(DEPTH, ncol),
        in_specs=[
            pl.BlockSpec((COND_ROWS, D_MODEL), lambda l, j: (0, 0)),
            pl.BlockSpec((None, D_MODEL, D_MODEL), lambda l, j: (l, 0, j)),
            pl.BlockSpec((None, 1, D_MODEL), lambda l, j: (l, 0, j)),
        ],
        out_specs=pl.BlockSpec((None, COND_ROWS, D_MODEL), lambda l, j: (l, 0, j)),
        out_shape=jax.ShapeDtypeStruct((DEPTH, COND_ROWS, 6 * D_MODEL), F32),
        compiler_params=_cparams(("arbitrary", "arbitrary")),
        name="ada_mod",
    )(cond, ada_w, ada_b.reshape(DEPTH, 1, 6 * D_MODEL))
    return out.reshape(DEPTH, COND_ROWS, 6, D_MODEL)


def _mod_spec(layer, tm):
    return pl.BlockSpec((None, None, 6, D_MODEL), lambda i, *_: (layer, _cond_of_token(i * tm), 0, 0))


def _l0_in_kernel(xp_ref, xs_ref, pos_ref, mod_ref, w_ref, xa_ref, xm_ref, z_ref):
    i = pl.program_id(0)
    x = jnp.where(i >= NB_CTX, xs_ref[...] + pos_ref[...], xp_ref[...])
    xa_ref[...] = x
    h = x * (1.0 + mod_ref[1:2, :]) + mod_ref[0:1, :]
    u = _bdot(h, w_ref[...])
    xm_ref[...] = u[:, :M_INNER]
    z_ref[...] = u[:, M_INNER:]


def _l0_in(xp, xs, pos, mod, w_up):
    return pl.pallas_call(
        _l0_in_kernel,
        grid=(NB,),
        in_specs=[
            pl.BlockSpec((TB, D_MODEL), lambda i: (jnp.minimum(i, NB_CTX - 1), 0)),
            pl.BlockSpec((TB, D_MODEL), lambda i: (jnp.maximum(i - NB_CTX, 0), 0)),
            pl.BlockSpec((TB, D_MODEL), lambda i: (jnp.maximum(i - NB_CTX, 0) % NB_SEQ, 0)),
            _mod_spec(0, TB),
            pl.BlockSpec((D_MODEL, 2 * M_INNER), lambda i: (0, 0)),
        ],
        out_specs=[
            pl.BlockSpec((TB, D_MODEL), lambda i: (i, 0)),
            pl.BlockSpec((TB, M_INNER), lambda i: (i, 0)),
            pl.BlockSpec((TB, M_INNER), lambda i: (i, 0)),
        ],
        out_shape=[
            jax.ShapeDtypeStruct((N_TOK, D_MODEL), F32),
            jax.ShapeDtypeStruct((N_TOK, M_INNER), F32),
            jax.ShapeDtypeStruct((N_TOK, M_INNER), F32),
        ],
        compiler_params=_cparams(("arbitrary",)),
        name="l0_in",
    )(xp, xs, pos, mod, w_up)


def _fill_ext(ext_ref, prev_ref, main_ref, next_ref):
    i = pl.program_id(0)
    ext_ref[0:HALO, :] = jnp.where(_seq_start(i), 0.0, prev_ref[...])
    ext_ref[HALO:HALO + TB, :] = main_ref[...]
    ext_ref[HALO + TB:HALO + TB + HALO, :] = jnp.where(_seq_end(i), 0.0, next_ref[...])


def _conv_cols(ext_ref, w_ref, b_ref, c0, c1):
    acc = b_ref[:, c0:c1]
    for j in range(CONV_W):
        r0 = HALO - CONV_W // 2 + j
        acc = acc + w_ref[j:j + 1, c0:c1] * ext_ref[r0:r0 + TB, c0:c1]
    return acc


def _halo_specs(width):
    nh = TB // HALO
    last = N_TOK // HALO - 1
    return [
        pl.BlockSpec((HALO, width), lambda i: (jnp.maximum(i * nh - 1, 0), 0)),
        pl.BlockSpec((TB, width), lambda i: (i, 0)),
        pl.BlockSpec((HALO, width), lambda i: (jnp.minimum((i + 1) * nh, last), 0)),
    ]


def _l0_qkv_kernel(prev_ref, xm_ref, next_ref, cw_ref, cb_ref, wq_ref, wk_ref, wv_ref, wg_ref, bg_ref,
                   xc_ref, q_ref, k_ref, v_ref, g_ref, bc_ref, ext_ref):
    _fill_ext(ext_ref, prev_ref, xm_ref, next_ref)
    pre = jnp.zeros((TB, LANES), F32) + bg_ref[...]
    for h in range(M_HEADS):
        c0, c1 = h * M_DH, (h + 1) * M_DH
        xc = _silu(_conv_cols(ext_ref, cw_ref, cb_ref, c0, c1))
        xc_ref[:, c0:c1] = xc
        xcb = xc.astype(BF16)
        q = jnp.dot(xcb, wq_ref[h], preferred_element_type=F32).astype(BF16)
        k = (jnp.dot(xcb, wk_ref[h], preferred_element_type=F32) * (M_DH ** -0.5)).astype(BF16)
        v = jnp.dot(xm_ref[:, c0:c1].astype(BF16), wv_ref[h], preferred_element_type=F32).astype(BF16)
        q_ref[:, c0:c1] = q
        k_ref[:, c0:c1] = k
        v_ref[:, c0:c1] = v
        pre = pre + jnp.dot(q, wg_ref[c0:c1, :], preferred_element_type=F32)
        pre = pre + jnp.dot(k, wg_ref[M_INNER + c0:M_INNER + c1, :], preferred_element_type=F32)
        pre = pre + jnp.dot(v, wg_ref[2 * M_INNER + c0:2 * M_INNER + c1, :], preferred_element_type=F32)
    lane = lax.broadcasted_iota(jnp.int32, (TB, LANES), 1)
    ng = N_DIR * M_HEADS
    logsig = -(jnp.maximum(-pre, 0.0) + jnp.log1p(jnp.exp(-jnp.abs(pre))))
    g = jnp.where(jnp.logical_and(lane >= ng, lane < 2 * ng), logsig, pre)
    g_ref[...] = g
    lower, upper = _tri_masks()
    lane_c = lax.broadcasted_iota(jnp.int32, (CHUNK, LANES), 1)
    fwd_lane = jnp.logical_and(lane_c >= ng, lane_c < ng + M_HEADS)
    for c in range(CPB):
        gc = g[c * CHUNK:(c + 1) * CHUNK, :]
        pref = _tri_dot(lower.astype(BF16), gc)
        suff = _tri_dot(upper.astype(BF16), gc)
        bc_ref[c * CHUNK:(c + 1) * CHUNK, :] = jnp.where(fwd_lane, pref, suff)


def _l0_qkv(xm, conv_w, conv_b, wq, wk, wv, wg, bg):
    full = lambda shape: pl.BlockSpec(shape, lambda i: (0,) * len(shape))
    tok = lambda w: pl.BlockSpec((TB, w), lambda i: (i, 0))
    return pl.pallas_call(
        _l0_qkv_kernel,
        grid=(NB,),
        in_specs=_halo_specs(M_INNER) + [
            full((CONV_W, M_INNER)), full((1, M_INNER)),
            full((M_HEADS, M_DH, M_DH)), full((M_HEADS, M_DH, M_DH)), full((M_HEADS, M_DH, M_DH)),
            full((3 * M_INNER, LANES)), full((1, LANES)),
        ],
        out_specs=[tok(M_INNER), tok(M_INNER), tok(M_INNER), tok(M_INNER), tok(LANES), tok(LANES)],
        out_shape=[
            jax.ShapeDtypeStruct((N_TOK, M_INNER), F32),
            jax.ShapeDtypeStruct((N_TOK, M_INNER), BF16),
            jax.ShapeDtypeStruct((N_TOK, M_INNER), BF16),
            jax.ShapeDtypeStruct((N_TOK, M_INNER), BF16),
            jax.ShapeDtypeStruct((N_TOK, LANES), F32),
            jax.ShapeDtypeStruct((N_TOK, LANES), F32),
        ],
        scratch_shapes=[pltpu.VMEM((TB + 2 * HALO, M_INNER), F32)],
        compiler_params=_cparams(("arbitrary",)),
        name="l0_qkv",
    )(xm, xm, xm, conv_w, conv_b, wq, wk, wv, wg, bg)


def _mlstm_chunk(r, q, k, v, gh, ght, c_ref, n_ref, m_ref):
    lower, upper = _tri_masks()
    mask = lower if r == 0 else upper
    li_col, li_row = gh[:, r:r + 1], ght[r:r + 1, :]
    b_col, b_row = gh[:, 2 + r:3 + r], ght[2 + r:3 + r, :]
    m_prev = m_ref[r]
    c_prev = c_ref[r]
    n_prev = n_ref[r]
    d = jnp.where(mask, b_col - b_row + li_row, -jnp.inf)
    g = b_col + m_prev
    m_t = jnp.maximum(g, jnp.max(d, axis=1, keepdims=True))
    qk = lax.dot_general(q, k, (((1,), (1,)), ((), ())), preferred_element_type=F32)
    s = qk * jnp.exp(d - m_t)
    inter = jnp.exp(g - m_t)
    qf = q.astype(F32)
    num = _bdot(s, v) + inter * _bdot(q, c_prev)
    den = jnp.sum(s, axis=1, keepdims=True) + inter * jnp.sum(qf * n_prev, axis=1, keepdims=True)
    h = num / jnp.maximum(jnp.abs(den), jnp.exp(-m_t))
    b_last = b_col[CHUNK - 1:CHUNK, :] if r == 0 else b_col[0:1, :]
    ld = b_last - b_col + li_col
    m_new = jnp.maximum(b_last + m_prev, jnp.max(ld, axis=0, keepdims=True))
    wk = jnp.exp(ld - m_new)
    decay = jnp.exp(b_last + m_prev - m_new)
    kw = k.astype(F32) * wk
    upd = lax.dot_general(kw.astype(BF16), v, (((0,), (0,)), ((), ())), preferred_element_type=F32)
    c_ref[r] = decay * c_prev + upd
    n_ref[r] = decay * n_prev + jnp.sum(kw, axis=0, keepdims=True)
    m_ref[r] = m_new
    return h


def _mlstm_kernel(qf_ref, kf_ref, vf_ref, ghf_ref, ghtf_ref, qb_ref, kb_ref, vb_ref, ghb_ref, ghtb_ref,
                  c0_ref, n0_ref, m0_ref,
                  hf_ref, hb_ref, co_ref, no_ref, mo_ref,
                  c_ref, n_ref, m_ref):
    s = pl.program_id(1)

    @pl.when(_seq_start(s))
    def _():
        is_ctx = s < NB_CTX
        c_ref[...] = jnp.where(is_ctx, 0.0, c0_ref[...])
        n_ref[...] = jnp.where(is_ctx, 0.0, n0_ref[...])
        m_ref[...] = jnp.where(is_ctx, 0.0, m0_ref[...])

    for c in range(CPB):
        rows = slice(c * CHUNK, (c + 1) * CHUNK)
        hf_ref[rows, :] = _mlstm_chunk(0, qf_ref[rows, :], kf_ref[rows, :], vf_ref[rows, :],
                                       ghf_ref[rows, :], ghtf_ref[:, rows], c_ref, n_ref, m_ref)
    for c in reversed(range(CPB)):
        rows = slice(c * CHUNK, (c + 1) * CHUNK)
        hb_ref[rows, :] = _mlstm_chunk(1, qb_ref[rows, :], kb_ref[rows, :], vb_ref[rows, :],
                                       ghb_ref[rows, :], ghtb_ref[:, rows], c_ref, n_ref, m_ref)

    @pl.when(s < NB_CTX)
    def _():
        co_ref[...] = c_ref[...]
        no_ref[...] = n_ref[...]
        mo_ref[...] = m_ref[...]


def _mlstm_scan(q, k, v, gh, ght, c0, n0, m0):
    fwd = lambda h, s: (s, h)
    bwd = lambda h, s: (_bwd_block(s), h)
    tokf = pl.BlockSpec((TB, M_DH), fwd)
    tokb = pl.BlockSpec((TB, M_DH), bwd)
    ghf = pl.BlockSpec((None, TB, 4), lambda h, s: (h, s, 0))
    ghb = pl.BlockSpec((None, TB, 4), lambda h, s: (h, _bwd_block(s), 0))
    ghtf = pl.BlockSpec((None, 4, TB), lambda h, s: (h, 0, s))
    ghtb = pl.BlockSpec((None, 4, TB), lambda h, s: (h, 0, _bwd_block(s)))
    lat = _lat_seq_of_block
    ctx = lambda s: jnp.minimum(s, NB_CTX - 1)
    return pl.pallas_call(
        _mlstm_kernel,
        grid=(M_HEADS, NB),
        in_specs=[
            tokf, tokf, tokf, ghf, ghtf, tokb, tokb, tokb, ghb, ghtb,
            pl.BlockSpec((None, None, N_DIR, None, M_DH, M_DH), lambda h, s: (lat(s), 0, 0, h, 0, 0)),
            pl.BlockSpec((None, N_DIR, None, 1, M_DH), lambda h, s: (lat(s), 0, h, 0, 0)),
            pl.BlockSpec((None, N_DIR, None, 1, 1), lambda h, s: (lat(s), 0, h, 0, 0)),
        ],
        out_specs=[
            tokf, tokb,
            pl.BlockSpec((None, None, N_DIR, None, M_DH, M_DH), lambda h, s: (ctx(s), 0, 0, h, 0, 0)),
            pl.BlockSpec((None, N_DIR, None, 1, M_DH), lambda h, s: (ctx(s), 0, h, 0, 0)),
            pl.BlockSpec((None, N_DIR, None, 1, 1), lambda h, s: (ctx(s), 0, h, 0, 0)),
        ],
        out_shape=[
            jax.ShapeDtypeStruct((N_TOK, M_INNER), F32),
            jax.ShapeDtypeStruct((N_TOK, M_INNER), F32),
            jax.ShapeDtypeStruct((BATCH, 1, N_DIR, M_HEADS, M_DH, M_DH), F32),
            jax.ShapeDtypeStruct((BATCH, N_DIR, M_HEADS, 1, M_DH), F32),
            jax.ShapeDtypeStruct((BATCH, N_DIR, M_HEADS, 1, 1), F32),
        ],
        scratch_shapes=[
            pltpu.VMEM((N_DIR, M_DH, M_DH), F32),
            pltpu.VMEM((N_DIR, 1, M_DH), F32),
            pltpu.VMEM((N_DIR, 1, 1), F32),
        ],
        compiler_params=_cparams(("arbitrary", "arbitrary")),
        name="mlstm_scan",
    )(q, k, v, gh, ght, q, k, v, gh, ght, c0, n0, m0)


def _post_ln(x, branch, gate, ln_g, ln_b):
    return _layer_norm(ALPHA * x + gate * branch, ln_g, ln_b)


def _l0_out_kernel(hf_ref, hb_ref, xc_ref, z_ref, nw_ref, sk_ref, w_ref, xa_ref, mod_ref, lg_ref, lb_ref,
                   x1_ref, h2_ref, o_scr):
    for h in range(M_HEADS):
        c0, c1 = h * M_DH, (h + 1) * M_DH
        hs = hf_ref[:, c0:c1] + hb_ref[:, c0:c1]
        mu = jnp.mean(hs, axis=-1, keepdims=True)
        d = hs - mu
        var = jnp.mean(d * d, axis=-1, keepdims=True)
        hn = d * lax.rsqrt(var + LN_EPS) * nw_ref[:, c0:c1]
        o = (hn + sk_ref[:, c0:c1] * xc_ref[:, c0:c1]) * _silu(z_ref[:, c0:c1])
        o_scr[:, c0:c1] = o.astype(BF16)
    out = jnp.dot(o_scr[...], w_ref[...], preferred_element_type=F32)
    x1 = _post_ln(xa_ref[...], out, mod_ref[2:3, :], lg_ref[...], lb_ref[...])
    x1_ref[...] = x1
    h2_ref[...] = (x1 * (1.0 + mod_ref[4:5, :]) + mod_ref[3:4, :]).astype(BF16)


def _l0_out(hf, hb, xc, z, norm_w, skip, w_down, xa, mod, ln_g, ln_b):
    full = lambda shape: pl.BlockSpec(shape, lambda i: (0,) * len(shape))
    tok = lambda w: pl.BlockSpec((TB, w), lambda i: (i, 0))
    return pl.pallas_call(
        _l0_out_kernel,
        grid=(NB,),
        in_specs=[tok(M_INNER), tok(M_INNER), tok(M_INNER), tok(M_INNER),
                  full((1, M_INNER)), full((1, M_INNER)), full((M_INNER, D_MODEL)),
                  tok(D_MODEL), _mod_spec(0, TB), full((1, D_MODEL)), full((1, D_MODEL))],
        out_specs=[tok(D_MODEL), tok(D_MODEL)],
        out_shape=[jax.ShapeDtypeStruct((N_TOK, D_MODEL), F32),
                   jax.ShapeDtypeStruct((N_TOK, D_MODEL), BF16)],
        scratch_shapes=[pltpu.VMEM((TB, M_INNER), BF16)],
        compiler_params=_cparams(("arbitrary",)),
        name="l0_out",
    )(hf, hb, xc, z, norm_w, skip, w_down, xa, mod, ln_g, ln_b)


def _ffn_kernel(h_ref, w1_ref, w3_ref, w2_ref, x_ref, mod_ref, lg_ref, lb_ref, modn_ref,
                x2_ref, h3_ref, acc_ref):
    f = pl.program_id(1)

    @pl.when(f == 0)
    def _():
        acc_ref[...] = jnp.zeros_like(acc_ref)

    h = h_ref[...]
    a = jnp.dot(h, w1_ref[...], preferred_element_type=F32)
    b = jnp.dot(h, w3_ref[...], preferred_element_type=F32)
    acc_ref[...] += _bdot(_silu(a) * b, w2_ref[...])

    @pl.when(f == pl.num_programs(1) - 1)
    def _():
        x2 = _post_ln(x_ref[...], acc_ref[...], mod_ref[5:6, :], lg_ref[...], lb_ref[...])
        x2_ref[...] = x2
        h3_ref[...] = (x2 * (1.0 + modn_ref[1:2, :]) + modn_ref[0:1, :]).astype(BF16)


def _ffn(h2, w1, w3, w2, x1, mod, ln_g, ln_b):
    tok = lambda w: pl.BlockSpec((TM_FF, w), lambda i, f: (i, 0))
    vec = pl.BlockSpec((1, D_MODEL), lambda i, f: (0, 0))
    return pl.pallas_call(
        _ffn_kernel,
        grid=(N_TOK // TM_FF, D_FF // TF_FF),
        in_specs=[tok(D_MODEL),
                  pl.BlockSpec((D_MODEL, TF_FF), lambda i, f: (0, f)),
                  pl.BlockSpec((D_MODEL, TF_FF), lambda i, f: (0, f)),
                  pl.BlockSpec((TF_FF, D_MODEL), lambda i, f: (f, 0)),
                  tok(D_MODEL), _mod_spec(0, TM_FF), vec, vec, _mod_spec(1, TM_FF)],
        out_specs=[tok(D_MODEL), tok(D_MODEL)],
        out_shape=[jax.ShapeDtypeStruct((N_TOK, D_MODEL), F32),
                   jax.ShapeDtypeStruct((N_TOK, D_MODEL), BF16)],
        scratch_shapes=[pltpu.VMEM((TM_FF, D_MODEL), F32)],
        compiler_params=_cparams(("arbitrary", "arbitrary")),
        name="ffn",
    )(h2, w1, w3, w2, x1, mod, ln_g, ln_b, mod)


def _l1_in_kernel(h_ref, wz_ref, wx_ref, wd_ref, z_ref, xbc_ref, dt_ref):
    h = h_ref[...]
    z_ref[...] = jnp.dot(h, wz_ref[...], preferred_element_type=F32)
    xbc_ref[...] = jnp.dot(h, wx_ref[...], preferred_element_type=F32)
    dt_ref[...] = jnp.dot(h, wd_ref[...], preferred_element_type=F32)


def _l1_in(h3, wz, wx, wd):
    full = lambda shape: pl.BlockSpec(shape, lambda i: (0,) * len(shape))
    tok = lambda w: pl.BlockSpec((TB, w), lambda i: (i, 0))
    return pl.pallas_call(
        _l1_in_kernel,
        grid=(NB,),
        in_specs=[tok(D_MODEL), full((D_MODEL, S_INNER)), full((D_MODEL, S_CONV_CH)), full((D_MODEL, LANES))],
        out_specs=[tok(S_INNER), tok(S_CONV_CH), tok(LANES)],
        out_shape=[jax.ShapeDtypeStruct((N_TOK, S_INNER), F32),
                   jax.ShapeDtypeStruct((N_TOK, S_CONV_CH), F32),
                   jax.ShapeDtypeStruct((N_TOK, LANES), F32)],
        compiler_params=_cparams(("arbitrary",)),
        name="l1_in",
    )(h3, wz, wx, wd)


def _l1_conv_kernel(prev_ref, xbc_ref, next_ref, cw_ref, cb_ref, dtr_ref, dtb_ref, a_ref,
                    xs_ref, bmt_ref, cm_ref, dt_ref, cum_ref, ext_ref):
    _fill_ext(ext_ref, prev_ref, xbc_ref, next_ref)
    cw = 512
    for j in range(S_INNER // cw):
        c0, c1 = j * cw, (j + 1) * cw
        xs_ref[:, c0:c1] = _silu(_conv_cols(ext_ref, cw_ref, cb_ref, c0, c1))
    for j in range(S_GN // cw):
        c0, c1 = S_INNER + j * cw, S_INNER + (j + 1) * cw
        bm = _silu(_conv_cols(ext_ref, cw_ref, cb_ref, c0, c1))
        bmt_ref[j * cw:(j + 1) * cw, :] = bm.T.astype(BF16)
        c0, c1 = c0 + S_GN, c1 + S_GN
        cm_ref[:, j * cw:(j + 1) * cw] = _silu(_conv_cols(ext_ref, cw_ref, cb_ref, c0, c1)).astype(BF16)
    x = dtr_ref[...] + dtb_ref[...]
    dt = jnp.maximum(x, 0.0) + jnp.log1p(jnp.exp(-jnp.abs(x)))
    dt_ref[...] = dt
    da = dt * a_ref[...]
    lower, upper = _tri_masks()
    lane_c = lax.broadcasted_iota(jnp.int32, (CHUNK, LANES), 1)
    for c in range(CPB):
        dc = da[c * CHUNK:(c + 1) * CHUNK, :]
        pref = _tri_dot(lower.astype(BF16), dc)
        suff = _tri_dot(upper.astype(BF16), dc)
        cum_ref[c * CHUNK:(c + 1) * CHUNK, :] = jnp.where(lane_c < S_HEADS, pref, suff)


def _l1_conv(xbc, conv_w, conv_b, dt_raw, dt_bias, a_neg):
    full = lambda shape: pl.BlockSpec(shape, lambda i: (0,) * len(shape))
    tok = lambda w: pl.BlockSpec((TB, w), lambda i: (i, 0))
    return pl.pallas_call(
        _l1_conv_kernel,
        grid=(NB,),
        in_specs=_halo_specs(S_CONV_CH) + [
            full((CONV_W, S_CONV_CH)), full((1, S_CONV_CH)), tok(LANES), full((1, LANES)), full((1, LANES))],
        out_specs=[tok(S_INNER), pl.BlockSpec((S_GN, TB), lambda i: (0, i)), tok(S_GN), tok(LANES), tok(LANES)],
        out_shape=[jax.ShapeDtypeStruct((N_TOK, S_INNER), F32),
                   jax.ShapeDtypeStruct((S_GN, N_TOK), BF16),
                   jax.ShapeDtypeStruct((N_TOK, S_GN), BF16),
                   jax.ShapeDtypeStruct((N_TOK, LANES), F32),
                   jax.ShapeDtypeStruct((N_TOK, LANES), F32)],
        scratch_shapes=[pltpu.VMEM((TB + 2 * HALO, S_CONV_CH), F32)],
        compiler_params=_cparams(("arbitrary",)),
        name="l1_conv",
    )(xbc, xbc, xbc, conv_w, conv_b, dt_raw, dt_bias, a_neg)


GW = S_HPG * S_HEADDIM


def _ssd_chunk(r, x, bmt, cm, dg, ct, dsk, st_ref, xw_ref, y_ref, rows):
    lower, upper = _tri_masks()
    mask = lower if r == 0 else upper
    st = st_ref[r]
    cb = jnp.dot(cm, bmt, preferred_element_type=F32)
    y2 = _bdot(cm, st)
    lane = lax.broadcasted_iota(jnp.int32, (1, GW), 1)
    dec = jnp.zeros((1, GW), F32)
    for j in range(S_HPG):
        c = S_HPG * r + j
        p0, p1 = j * S_HEADDIM, (j + 1) * S_HEADDIM
        dt_col = dg[:, c:c + 1]
        cum_col = dg[:, 2 * S_HPG + c:2 * S_HPG + c + 1]
        cum_row = ct[c:c + 1, :]
        decay = jnp.exp(jnp.where(mask, cum_col - cum_row, -jnp.inf))
        xj = x[:, p0:p1]
        y1 = _bdot(decay * cb, xj * dt_col)
        yj = y1 + jnp.exp(cum_col) * y2[:, p0:p1]
        if r == 0:
            yj = yj + dsk[:, p0:p1] * xj
        y_ref[rows, p0:p1] = yj
        last = cum_col[CHUNK - 1:CHUNK, :] if r == 0 else cum_col[0:1, :]
        xw_ref[:, p0:p1] = xj * (jnp.exp(last - cum_col) * dt_col)
        dec = jnp.where(jnp.logical_and(lane >= p0, lane < p1), jnp.exp(last), dec)
    st_ref[r] = dec * st + _bdot(bmt, xw_ref[...])


def _ssd_kernel(xf_ref, bf_ref, cf_ref, dgf_ref, ctf_ref, xb_ref, bb_ref, cb_ref, dgb_ref, ctb_ref,
                dsk_ref, s0_ref, yf_ref, yb_ref, so_ref, st_ref, xw_ref):
    s = pl.program_id(1)

    @pl.when(_seq_start(s))
    def _():
        is_ctx = s < NB_CTX
        for r in range(N_DIR):
            s0 = jnp.concatenate([s0_ref[r, j] for j in range(S_HPG)], axis=0)
            st_ref[r] = jnp.where(is_ctx, 0.0, s0.T)

    dsk = dsk_ref[...]
    for c in range(CPB):
        rows = slice(c * CHUNK, (c + 1) * CHUNK)
        _ssd_chunk(0, xf_ref[rows, :], bf_ref[:, rows], cf_ref[rows, :], dgf_ref[rows, :], ctf_ref[:, rows],
                   dsk, st_ref, xw_ref, yf_ref, rows)
    for c in reversed(range(CPB)):
        rows = slice(c * CHUNK, (c + 1) * CHUNK)
        _ssd_chunk(1, xb_ref[rows, :], bb_ref[:, rows], cb_ref[rows, :], dgb_ref[rows, :], ctb_ref[:, rows],
                   dsk, st_ref, xw_ref, yb_ref, rows)

    @pl.when(s < NB_CTX)
    def _():
        for r in range(N_DIR):
            stt = st_ref[r].T
            for j in range(S_HPG):
                so_ref[r, j] = stt[j * S_HEADDIM:(j + 1) * S_HEADDIM, :]


def _ssd_scan(xs, bmt, cm, dg, ct, dsk, s0):
    ng = 2 * N_DIR * S_HPG
    def side(blk):
        return [
            pl.BlockSpec((TB, GW), lambda g, s: (blk(s), g)),
            pl.BlockSpec((S_STATE, TB), lambda g, s: (g, blk(s))),
            pl.BlockSpec((TB, S_STATE), lambda g, s: (blk(s), g)),
            pl.BlockSpec((None, TB, ng), lambda g, s: (g, blk(s), 0)),
            pl.BlockSpec((None, N_DIR * S_HPG, TB), lambda g, s: (g, 0, blk(s))),
        ]
    ident = lambda s: s
    lat = _lat_seq_of_block
    ctx = lambda s: jnp.minimum(s, NB_CTX - 1)
    st_block = (None, None, N_DIR, S_HPG, S_HEADDIM, S_STATE)
    return pl.pallas_call(
        _ssd_kernel,
        grid=(S_GROUPS, NB),
        in_specs=side(ident) + side(_bwd_block) + [
            pl.BlockSpec((None, 1, GW), lambda g, s: (g, 0, 0)),
            pl.BlockSpec(st_block, lambda g, s: (lat(s), 0, 0, g, 0, 0)),
        ],
        out_specs=[
            pl.BlockSpec((TB, GW), lambda g, s: (s, g)),
            pl.BlockSpec((TB, GW), lambda g, s: (_bwd_block(s), g)),
            pl.BlockSpec(st_block, lambda g, s: (ctx(s), 0, 0, g, 0, 0)),
        ],
        out_shape=[
            jax.ShapeDtypeStruct((N_TOK, S_INNER), F32),
            jax.ShapeDtypeStruct((N_TOK, S_INNER), F32),
            jax.ShapeDtypeStruct((BATCH, 1, N_DIR, S_HEADS, S_HEADDIM, S_STATE), F32),
        ],
        scratch_shapes=[pltpu.VMEM((N_DIR, S_STATE, GW), F32), pltpu.VMEM((CHUNK, GW), F32)],
        compiler_params=_cparams(("arbitrary", "arbitrary")),
        name="ssd_scan",
    )(xs, bmt, cm, dg, ct, xs, bmt, cm, dg, ct, dsk, s0)


def _l1_out_kernel(yf_ref, yb_ref, z_ref, nw_ref, w_ref, x_ref, mod_ref, lg_ref, lb_ref, wr_ref,
                   x3_ref, h4_ref, gates_ref):
    y = (yf_ref[...] + yb_ref[...]) * _silu(z_ref[...])
    y = y * lax.rsqrt(jnp.mean(y * y, axis=-1, keepdims=True) + RMS_EPS) * nw_ref[...]
    out = _bdot(y, w_ref[...])
    x3 = _post_ln(x_ref[...], out, mod_ref[2:3, :], lg_ref[...], lb_ref[...])
    x3_ref[...] = x3
    h4 = x3 * (1.0 + mod_ref[4:5, :]) + mod_ref[3:4, :]
    h4_ref[...] = h4.astype(BF16)
    lane = lax.broadcasted_iota(jnp.int32, (TB, LANES), 1)
    logits = jnp.full((TB, LANES), -jnp.inf, F32)
    for e in range(N_EXPERTS):
        le = jnp.sum(h4 * wr_ref[e:e + 1, :], axis=-1, keepdims=True)
        logits = jnp.where(lane == e, le, logits)
    v1 = jnp.max(logits, axis=-1, keepdims=True)
    i1 = jnp.min(jnp.where(logits == v1, lane, LANES), axis=-1, keepdims=True)
    rest = jnp.where(lane == i1, -jnp.inf, logits)
    v2 = jnp.max(rest, axis=-1, keepdims=True)
    i2 = jnp.min(jnp.where(rest == v2, lane, LANES), axis=-1, keepdims=True)
    e2 = jnp.exp(v2 - v1)
    w1 = 1.0 / (1.0 + e2)
    w2 = e2 / (1.0 + e2)
    gates_ref[...] = jnp.where(lane == i1, w1, 0.0) + jnp.where(lane == i2, w2, 0.0)


def _l1_out(yf, yb, z, norm_w, w_out, x2, mod, ln_g, ln_b, wr_t):
    full = lambda shape: pl.BlockSpec(shape, lambda i: (0,) * len(shape))
    tok = lambda w: pl.BlockSpec((TB, w), lambda i: (i, 0))
    return pl.pallas_call(
        _l1_out_kernel,
        grid=(NB,),
        in_specs=[tok(S_INNER), tok(S_INNER), tok(S_INNER), full((1, S_INNER)), full((S_INNER, D_MODEL)),
                  tok(D_MODEL), _mod_spec(1, TB), full((1, D_MODEL)), full((1, D_MODEL)),
                  full((N_EXPERTS, D_MODEL))],
        out_specs=[tok(D_MODEL), tok(D_MODEL), tok(LANES)],
        out_shape=[jax.ShapeDtypeStruct((N_TOK, D_MODEL), F32),
                   jax.ShapeDtypeStruct((N_TOK, D_MODEL), BF16),
                   jax.ShapeDtypeStruct((N_TOK, LANES), F32)],
        compiler_params=_cparams(("arbitrary",)),
        name="l1_out",
    )(yf, yb, z, norm_w, w_out, x2, mod, ln_g, ln_b, wr_t)


def _moe_kernel(h_ref, g_ref, w1_ref, w3_ref, w2_ref, x_ref, mod_ref, lg_ref, lb_ref,
                yp_ref, ys_ref, acc_ref):
    i, e, f = pl.program_id(0), pl.program_id(1), pl.program_id(2)

    @pl.when(jnp.logical_and(e == 0, f == 0))
    def _():
        acc_ref[...] = jnp.zeros_like(acc_ref)

    h = h_ref[...]
    a = jnp.dot(h, w1_ref[...], preferred_element_type=F32)
    b = jnp.dot(h, w3_ref[...], preferred_element_type=F32)
    lane = lax.broadcasted_iota(jnp.int32, (TM_FF, LANES), 1)
    ge = jnp.sum(jnp.where(lane == e, g_ref[...], 0.0), axis=-1, keepdims=True)
    acc_ref[...] += ge * _bdot(_silu(a) * b, w2_ref[...])

    last = jnp.logical_and(e == pl.num_programs(1) - 1, f == pl.num_programs(2) - 1)

    @pl.when(last)
    def _():
        y = _post_ln(x_ref[...], acc_ref[...], mod_ref[5:6, :], lg_ref[...], lb_ref[...])

        @pl.when(i < N_CTX // TM_FF)
        def _():
            yp_ref[...] = y

        @pl.when(i >= N_CTX // TM_FF)
        def _():
            ys_ref[...] = y


def _moe(h4, gates, w1, w3, w2, x3, mod, ln_g, ln_b):
    nctx = N_CTX // TM_FF
    tok = lambda w: pl.BlockSpec((TM_FF, w), lambda i, e, f: (i, 0))
    vec = pl.BlockSpec((1, D_MODEL), lambda i, e, f: (0, 0))
    return pl.pallas_call(
        _moe_kernel,
        grid=(N_TOK // TM_FF, N_EXPERTS, E_FF // TF_MOE),
        in_specs=[tok(D_MODEL), tok(LANES),
                  pl.BlockSpec((None, D_MODEL, TF_MOE), lambda i, e, f: (e, 0, f)),
                  pl.BlockSpec((None, D_MODEL, TF_MOE), lambda i, e, f: (e, 0, f)),
                  pl.BlockSpec((None, TF_MOE, D_MODEL), lambda i, e, f: (e, f, 0)),
                  tok(D_MODEL), _mod_spec(1, TM_FF), vec, vec],
        out_specs=[pl.BlockSpec((TM_FF, D_MODEL), lambda i, e, f: (jnp.minimum(i, nctx - 1), 0)),
                   pl.BlockSpec((TM_FF, D_MODEL), lambda i, e, f: (jnp.maximum(i - nctx, 0), 0))],
        out_shape=[jax.ShapeDtypeStruct((N_CTX, D_MODEL), F32),
                   jax.ShapeDtypeStruct((N_LAT, D_MODEL), F32)],
        scratch_shapes=[pltpu.VMEM((TM_FF, D_MODEL), F32)],
        compiler_params=_cparams(("arbitrary", "arbitrary", "arbitrary")),
        name="moe",
    )(h4, gates, w1, w3, w2, x3, mod, ln_g, ln_b)


def _grid_pos_embed():
    rows = DEC_SEQ // GRID_W
    quarter = D_MODEL // 4
    freqs = jnp.exp(-math.log(POS_BASE) * jnp.arange(quarter, dtype=F32) / quarter)
    er = jnp.arange(rows, dtype=F32)[:, None] * freqs
    ec = jnp.arange(GRID_W, dtype=F32)[:, None] * freqs
    row_emb = jnp.concatenate([jnp.sin(er), jnp.cos(er)], axis=-1)
    col_emb = jnp.concatenate([jnp.sin(ec), jnp.cos(ec)], axis=-1)
    pos = jnp.concatenate([jnp.broadcast_to(row_emb[:, None, :], (rows, GRID_W, D_MODEL // 2)),
                           jnp.broadcast_to(col_emb[None, :, :], (rows, GRID_W, D_MODEL // 2))], axis=-1)
    return pos.reshape(DEC_SEQ, D_MODEL)


def _pad_lanes(a, width=LANES):
    return jnp.pad(a, [(0, 0)] * (a.ndim - 1) + [(0, width - a.shape[-1])])


def kernel(x_prompt, x_sample, c, state_mlstm_C, state_mlstm_n, state_mlstm_m, state_ssm, c_ctx,
           ada_w, ada_b, ln_g, ln_b,
           ml_w_up, ml_conv_w, ml_conv_b, ml_wq, ml_wk, ml_wv, ml_w_ig, ml_b_ig, ml_w_fg, ml_b_fg,
           ml_norm_w, ml_skip, ml_w_down,
           ss_w_in, ss_conv_w, ss_conv_b, ss_dt_bias, ss_a_log, ss_d, ss_norm_w, ss_w_out,
           ff_w1, ff_w3, ff_w2,
           moe_w_router, moe_w1, moe_w3, moe_w2):
    row = lambda a: a.reshape(1, -1)
    cond = jnp.concatenate([c_ctx[None, :], c, jnp.zeros((COND_ROWS - N_COND, D_MODEL), F32)], axis=0)
    mod = _ada_mod(cond, ada_w.astype(BF16), ada_b)

    xa, xm, z0 = _l0_in(x_prompt.reshape(N_CTX, D_MODEL), x_sample.reshape(N_LAT, D_MODEL), _grid_pos_embed(),
                        mod, ml_w_up[0].astype(BF16))
    ng = N_DIR * M_HEADS
    wg = jnp.concatenate([ml_w_ig[0].transpose(1, 0, 2).reshape(3 * M_INNER, ng),
                          ml_w_fg[0].transpose(1, 0, 2).reshape(3 * M_INNER, ng)], axis=1)
    bg = jnp.concatenate([ml_b_ig[0].reshape(ng), ml_b_fg[0].reshape(ng)])
    xc, q, k, v, g, bc = _l0_qkv(xm, ml_conv_w[0], row(ml_conv_b[0]),
                                 ml_wq[0].astype(BF16), ml_wk[0].astype(BF16), ml_wv[0].astype(BF16),
                                 _pad_lanes(wg).astype(BF16), row(_pad_lanes(bg)))
    gh = jnp.concatenate([g[:, :ng].reshape(N_TOK, N_DIR, M_HEADS),
                          bc[:, ng:2 * ng].reshape(N_TOK, N_DIR, M_HEADS)], axis=1)
    ght = gh.transpose(2, 1, 0)
    gh = gh.transpose(2, 0, 1)
    hf, hb, new_c, new_n, new_m = _mlstm_scan(
        q, k, v, gh, ght, state_mlstm_C,
        state_mlstm_n.reshape(DEC_BATCH, N_DIR, M_HEADS, 1, M_DH),
        state_mlstm_m.reshape(DEC_BATCH, N_DIR, M_HEADS, 1, 1))
    x1, h2 = _l0_out(hf, hb, xc, z0, row(ml_norm_w[0]), row(ml_skip[0]), ml_w_down[0].astype(BF16),
                     xa, mod, row(ln_g[0, 0]), row(ln_b[0, 0]))
    x2, h3 = _ffn(h2, ff_w1[0].astype(BF16), ff_w3[0].astype(BF16), ff_w2[0].astype(BF16),
                  x1, mod, row(ln_g[0, 1]), row(ln_b[0, 1]))

    w_in = ss_w_in[0]
    z1, xbc, dt_raw = _l1_in(h3, w_in[:, :S_INNER].astype(BF16),
                             w_in[:, S_INNER:S_INNER + S_CONV_CH].astype(BF16),
                             _pad_lanes(w_in[:, S_INNER + S_CONV_CH:]).astype(BF16))
    a_neg = -jnp.exp(ss_a_log[0].reshape(N_DIR * S_HEADS))
    xs, bmt, cm, dt, cum = _l1_conv(xbc, ss_conv_w[0], row(ss_conv_b[0]), dt_raw,
                                    row(_pad_lanes(ss_dt_bias[0].reshape(N_DIR * S_HEADS))), row(_pad_lanes(a_neg)))
    nh = N_DIR * S_HEADS
    to_group = lambda a: a[:, :nh].reshape(N_TOK, N_DIR, S_GROUPS, S_HPG).transpose(2, 0, 1, 3).reshape(
        S_GROUPS, N_TOK, N_DIR * S_HPG)
    cum_g = to_group(cum)
    dg = jnp.concatenate([to_group(dt), cum_g], axis=-1)
    ct = cum_g.transpose(0, 2, 1)
    dsk = jnp.repeat(ss_d[0], S_HEADDIM).reshape(S_GROUPS, 1, GW)
    yf, yb, new_s = _ssd_scan(xs, bmt, cm, dg, ct, dsk, state_ssm)
    x3, h4, gates = _l1_out(yf, yb, z1, row(ss_norm_w[0]), ss_w_out[0].astype(BF16), x2, mod,
                            row(ln_g[1, 0]), row(ln_b[1, 0]), moe_w_router[0].T)
    yp, ys = _moe(h4, gates, moe_w1[0].astype(BF16), moe_w3[0].astype(BF16), moe_w2[0].astype(BF16),
                  x3, mod, row(ln_g[1, 1]), row(ln_b[1, 1]))

    return (yp.reshape(BATCH, SEQ, D_MODEL), ys.reshape(DEC_BATCH, DEC_SEQ, D_MODEL),
            new_c,
            new_n.reshape(BATCH, 1, N_DIR, M_HEADS, M_DH),
            new_m.reshape(BATCH, 1, N_DIR, M_HEADS),
            new_s)
```

```python
import functools
import math

import jax
import jax.numpy as jnp
from jax import lax
from jax.experimental import pallas as pl
from jax.experimental.pallas import tpu as pltpu

F32 = jnp.float32
BF16 = jnp.bfloat16

D_MODEL = 1024
BATCH = 16
SEQ = 256
DEC_BATCH = 4
DEC_SEQ = 4096
GRID_W = 64
CHUNK = 128
CONV_W = 5
N_DIR = 2
M_INNER = 2 * D_MODEL
M_HEADS = 4
M_DH = M_INNER // M_HEADS
S_INNER = 2 * D_MODEL
S_HEADDIM = 64
S_HEADS = S_INNER // S_HEADDIM
S_GROUPS = 8
S_HPG = S_HEADS // S_GROUPS
S_STATE = 128
S_GN = S_GROUPS * S_STATE
S_CONV_CH = S_INNER + 2 * S_GN
D_FF = 2816
N_EXPERTS = 8
TOP_K = 2
E_FF = 3584
DEPTH = 2
ALPHA = (2 * DEPTH) ** 0.25
LN_EPS = 1e-5
RMS_EPS = 1e-5
POS_BASE = 10000.0

N_CTX = BATCH * SEQ
N_LAT = DEC_BATCH * DEC_SEQ
N_TOK = N_CTX + N_LAT
N_COND = 1 + DEC_BATCH
COND_ROWS = 8

TB = 256
NB = N_TOK // TB
NB_CTX = N_CTX // TB
NB_SEQ = DEC_SEQ // TB
HALO = 8
CPB = TB // CHUNK

TM_FF = 512
TF_FF = 1408
TF_MOE = 1792
LANES = 128

VMEM_LIMIT = 56 * 1024 * 1024


def _cparams(sem):
    return pltpu.CompilerParams(dimension_semantics=sem, vmem_limit_bytes=VMEM_LIMIT)


def _cond_of_token(t0):
    return jnp.where(t0 < N_CTX, 0, 1 + (t0 - N_CTX) // DEC_SEQ)


def _lat_seq_of_block(s):
    return jnp.maximum(s - NB_CTX, 0) // NB_SEQ


def _bwd_block(s):
    k = s - NB_CTX
    return jnp.where(s < NB_CTX, s, NB_CTX + (k // NB_SEQ) * NB_SEQ + (NB_SEQ - 1 - k % NB_SEQ))


def _seq_start(s):
    return jnp.logical_or(s < NB_CTX, (s - NB_CTX) % NB_SEQ == 0)


def _seq_end(s):
    return jnp.logical_or(s < NB_CTX, (s - NB_CTX) % NB_SEQ == NB_SEQ - 1)


def _silu(x):
    return x * jax.nn.sigmoid(x)


def _bdot(a, b):
    return jnp.dot(a.astype(BF16), b.astype(BF16), preferred_element_type=F32)


def _tri_dot(tri, x):
    hi = x.astype(BF16)
    r1 = x - hi.astype(F32)
    mid = r1.astype(BF16)
    lo = (r1 - mid.astype(F32)).astype(BF16)
    d = lambda a: jnp.dot(tri, a, preferred_element_type=F32)
    return d(hi) + d(mid) + d(lo)


def _tri_masks():
    t = lax.broadcasted_iota(jnp.int32, (CHUNK, CHUNK), 0)
    u = lax.broadcasted_iota(jnp.int32, (CHUNK, CHUNK), 1)
    return u <= t, u >= t


def _layer_norm(r, g, b):
    mu = jnp.mean(r, axis=-1, keepdims=True)
    d = r - mu
    var = jnp.mean(d * d, axis=-1, keepdims=True)
    return d * lax.rsqrt(var + LN_EPS) * g + b


def _ada_kernel(cond_ref, w_ref, b_ref, o_ref):
    o_ref[...] = _bdot(_silu(cond_ref[...]), w_ref[...]) + b_ref[...]


def _ada_mod(cond, ada_w, ada_b):
    ncol = 6 * D_MODEL // D_MODEL
    out = pl.pallas_call(
        _ada_kernel,
        grid=(DEPTH, ncol),
        in_specs=[
            pl.BlockSpec((COND_ROWS, D_MODEL), lambda l, j: (0, 0)),
            pl.BlockSpec((None, D_MODEL, D_MODEL), lambda l, j: (l, 0, j)),
            pl.BlockSpec((None, 1, D_MODEL), lambda l, j: (l, 0, j)),
        ],
        out_specs=pl.BlockSpec((None, COND_ROWS, D_MODEL), lambda l, j: (l, 0, j)),
        out_shape=jax.ShapeDtypeStruct((DEPTH, COND_ROWS, 6 * D_MODEL), F32),
        compiler_params=_cparams(("arbitrary", "arbitrary")),
        name="ada_mod",
    )(cond, ada_w, ada_b.reshape(DEPTH, 1, 6 * D_MODEL))
    return out.reshape(DEPTH, COND_ROWS, 6, D_MODEL)


def _mod_spec(layer, tm):
    return pl.BlockSpec((None, None, 6, D_MODEL), lambda i, *_: (layer, _cond_of_token(i * tm), 0, 0))


def _l0_in_kernel(xp_ref, xs_ref, pos_ref, mod_ref, w_ref, xa_ref, xm_ref, z_ref):
    i = pl.program_id(0)
    x = jnp.where(i >= NB_CTX, xs_ref[...] + pos_ref[...], xp_ref[...])
    xa_ref[...] = x
    h = x * (1.0 + mod_ref[1:2, :]) + mod_ref[0:1, :]
    u = _bdot(h, w_ref[...])
    xm_ref[...] = u[:, :M_INNER]
    z_ref[...] = u[:, M_INNER:]


def _l0_in(xp, xs, pos, mod, w_up):
    return pl.pallas_call(
        _l0_in_kernel,
        grid=(NB,),
        in_specs=[
            pl.BlockSpec((TB, D_MODEL), lambda i: (jnp.minimum(i, NB_CTX - 1), 0)),
            pl.BlockSpec((TB, D_MODEL), lambda i: (jnp.maximum(i - NB_CTX, 0), 0)),
            pl.BlockSpec((TB, D_MODEL), lambda i: (jnp.maximum(i - NB_CTX, 0) % NB_SEQ, 0)),
            _mod_spec(0, TB),
            pl.BlockSpec((D_MODEL, 2 * M_INNER), lambda i: (0, 0)),
        ],
        out_specs=[
            pl.BlockSpec((TB, D_MODEL), lambda i: (i, 0)),
            pl.BlockSpec((TB, M_INNER), lambda i: (i, 0)),
            pl.BlockSpec((TB, M_INNER), lambda i: (i, 0)),
        ],
        out_shape=[
            jax.ShapeDtypeStruct((N_TOK, D_MODEL), F32),
            jax.ShapeDtypeStruct((N_TOK, M_INNER), F32),
            jax.ShapeDtypeStruct((N_TOK, M_INNER), F32),
        ],
        compiler_params=_cparams(("arbitrary",)),
        name="l0_in",
    )(xp, xs, pos, mod, w_up)


def _fill_ext(ext_ref, prev_ref, main_ref, next_ref):
    i = pl.program_id(0)
    ext_ref[0:HALO, :] = jnp.where(_seq_start(i), 0.0, prev_ref[...])
    ext_ref[HALO:HALO + TB, :] = main_ref[...]
    ext_ref[HALO + TB:HALO + TB + HALO, :] = jnp.where(_seq_end(i), 0.0, next_ref[...])


def _conv_cols(ext_ref, w_ref, b_ref, c0, c1):
    acc = b_ref[:, c0:c1]
    for j in range(CONV_W):
        r0 = HALO - CONV_W // 2 + j
        acc = acc + w_ref[j:j + 1, c0:c1] * ext_ref[r0:r0 + TB, c0:c1]
    return acc


def _halo_specs(width):
    nh = TB // HALO
    last = N_TOK // HALO - 1
    return [
        pl.BlockSpec((HALO, width), lambda i: (jnp.maximum(i * nh - 1, 0), 0)),
        pl.BlockSpec((TB, width), lambda i: (i, 0)),
        pl.BlockSpec((HALO, width), lambda i: (jnp.minimum((i + 1) * nh, last), 0)),
    ]


def _l0_qkv_kernel(prev_ref, xm_ref, next_ref, cw_ref, cb_ref, wq_ref, wk_ref, wv_ref, wg_ref, bg_ref,
                   xc_ref, q_ref, k_ref, v_ref, g_ref, bc_ref, ext_ref):
    _fill_ext(ext_ref, prev_ref, xm_ref, next_ref)
    pre = jnp.zeros((TB, LANES), F32) + bg_ref[...]
    for h in range(M_HEADS):
        c0, c1 = h * M_DH, (h + 1) * M_DH
        xc = _silu(_conv_cols(ext_ref, cw_ref, cb_ref, c0, c1))
        xc_ref[:, c0:c1] = xc
        xcb = xc.astype(BF16)
        q = jnp.dot(xcb, wq_ref[h], preferred_element_type=F32).astype(BF16)
        k = (jnp.dot(xcb, wk_ref[h], preferred_element_type=F32) * (M_DH ** -0.5)).astype(BF16)
        v = jnp.dot(xm_ref[:, c0:c1].astype(BF16), wv_ref[h], preferred_element_type=F32).astype(BF16)
        q_ref[:, c0:c1] = q
        k_ref[:, c0:c1] = k
        v_ref[:, c0:c1] = v
        pre = pre + jnp.dot(q, wg_ref[c0:c1, :], preferred_element_type=F32)
        pre = pre + jnp.dot(k, wg_ref[M_INNER + c0:M_INNER + c1, :], preferred_element_type=F32)
        pre = pre + jnp.dot(v, wg_ref[2 * M_INNER + c0:2 * M_INNER + c1, :], preferred_element_type=F32)
    lane = lax.broadcasted_iota(jnp.int32, (TB, LANES), 1)
    ng = N_DIR * M_HEADS
    logsig = -(jnp.maximum(-pre, 0.0) + jnp.log1p(jnp.exp(-jnp.abs(pre))))
    g = jnp.where(jnp.logical_and(lane >= ng, lane < 2 * ng), logsig, pre)
    g_ref[...] = g
    lower, upper = _tri_masks()
    lane_c = lax.broadcasted_iota(jnp.int32, (CHUNK, LANES), 1)
    fwd_lane = jnp.logical_and(lane_c >= ng, lane_c < ng + M_HEADS)
    for c in range(CPB):
        gc = g[c * CHUNK:(c + 1) * CHUNK, :]
        pref = _tri_dot(lower.astype(BF16), gc)
        suff = _tri_dot(upper.astype(BF16), gc)
        bc_ref[c * CHUNK:(c + 1) * CHUNK, :] = jnp.where(fwd_lane, pref, suff)


def _l0_qkv(xm, conv_w, conv_b, wq, wk, wv, wg, bg):
    full = lambda shape: pl.BlockSpec(shape, lambda i: (0,) * len(shape))
    tok = lambda w: pl.BlockSpec((TB, w), lambda i: (i, 0))
    return pl.pallas_call(
        _l0_qkv_kernel,
        grid=(NB,),
        in_specs=_halo_specs(M_INNER) + [
            full((CONV_W, M_INNER)), full((1, M_INNER)),
            full((M_HEADS, M_DH, M_DH)), full((M_HEADS, M_DH, M_DH)), full((M_HEADS, M_DH, M_DH)),
            full((3 * M_INNER, LANES)), full((1, LANES)),
        ],
        out_specs=[tok(M_INNER), tok(M_INNER), tok(M_INNER), tok(M_INNER), tok(LANES), tok(LANES)],
        out_shape=[
            jax.ShapeDtypeStruct((N_TOK, M_INNER), F32),
            jax.ShapeDtypeStruct((N_TOK, M_INNER), BF16),
            jax.ShapeDtypeStruct((N_TOK, M_INNER), BF16),
            jax.ShapeDtypeStruct((N_TOK, M_INNER), BF16),
            jax.ShapeDtypeStruct((N_TOK, LANES), F32),
            jax.ShapeDtypeStruct((N_TOK, LANES), F32),
        ],
        scratch_shapes=[pltpu.VMEM((TB + 2 * HALO, M_INNER), F32)],
        compiler_params=_cparams(("arbitrary",)),
        name="l0_qkv",
    )(xm, xm, xm, conv_w, conv_b, wq, wk, wv, wg, bg)


def _mlstm_chunk(r, q, k, v, gh, ght, c_ref, n_ref, m_ref):
    lower, upper = _tri_masks()
    mask = lower if r == 0 else upper
    li_col, li_row = gh[:, r:r + 1], ght[r:r + 1, :]
    b_col, b_row = gh[:, 2 + r:3 + r], ght[2 + r:3 + r, :]
    m_prev = m_ref[r]
    c_prev = c_ref[r]
    n_prev = n_ref[r]
    d = jnp.where(mask, b_col - b_row + li_row, -jnp.inf)
    g = b_col + m_prev
    m_t = jnp.maximum(g, jnp.max(d, axis=1, keepdims=True))
    qk = lax.dot_general(q, k, (((1,), (1,)), ((), ())), preferred_element_type=F32)
    s = qk * jnp.exp(d - m_t)
    inter = jnp.exp(g - m_t)
    qf = q.astype(F32)
    num = _bdot(s, v) + inter * _bdot(q, c_prev)
    den = jnp.sum(s, axis=1, keepdims=True) + inter * jnp.sum(qf * n_prev, axis=1, keepdims=True)
    h = num / jnp.maximum(jnp.abs(den), jnp.exp(-m_t))
    b_last = b_col[CHUNK - 1:CHUNK, :] if r == 0 else b_col[0:1, :]
    ld = b_last - b_col + li_col
    m_new = jnp.maximum(b_last + m_prev, jnp.max(ld, axis=0, keepdims=True))
    wk = jnp.exp(ld - m_new)
    decay = jnp.exp(b_last + m_prev - m_new)
    kw = k.astype(F32) * wk
    upd = lax.dot_general(kw.astype(BF16), v, (((0,), (0,)), ((), ())), preferred_element_type=F32)
    c_ref[r] = decay * c_prev + upd
    n_ref[r] = decay * n_prev + jnp.sum(kw, axis=0, keepdims=True)
    m_ref[r] = m_new
    return h


def _mlstm_kernel(qf_ref, kf_ref, vf_ref, ghf_ref, ghtf_ref, qb_ref, kb_ref, vb_ref, ghb_ref, ghtb_ref,
                  c0_ref, n0_ref, m0_ref,
                  hf_ref, hb_ref, co_ref, no_ref, mo_ref,
                  c_ref, n_ref, m_ref):
    s = pl.program_id(1)

    @pl.when(_seq_start(s))
    def _():
        is_ctx = s < NB_CTX
        c_ref[...] = jnp.where(is_ctx, 0.0, c0_ref[...])
        n_ref[...] = jnp.where(is_ctx, 0.0, n0_ref[...])
        m_ref[...] = jnp.where(is_ctx, 0.0, m0_ref[...])

    for c in range(CPB):
        rows = slice(c * CHUNK, (c + 1) * CHUNK)
        hf_ref[rows, :] = _mlstm_chunk(0, qf_ref[rows, :], kf_ref[rows, :], vf_ref[rows, :],
                                       ghf_ref[rows, :], ghtf_ref[:, rows], c_ref, n_ref, m_ref)
    for c in reversed(range(CPB)):
        rows = slice(c * CHUNK, (c + 1) * CHUNK)
        hb_ref[rows, :] = _mlstm_chunk(1, qb_ref[rows, :], kb_ref[rows, :], vb_ref[rows, :],
                                       ghb_ref[rows, :], ghtb_ref[:, rows], c_ref, n_ref, m_ref)

    @pl.when(s < NB_CTX)
    def _():
        co_ref[...] = c_ref[...]
        no_ref[...] = n_ref[...]
        mo_ref[...] = m_ref[...]


def _mlstm_scan(q, k, v, gh, ght, c0, n0, m0):
    fwd = lambda h, s: (s, h)
    bwd = lambda h, s: (_bwd_block(s), h)
    tokf = pl.BlockSpec((TB, M_DH), fwd)
    tokb = pl.BlockSpec((TB, M_DH), bwd)
    ghf = pl.BlockSpec((None, TB, 4), lambda h, s: (h, s, 0))
    ghb = pl.BlockSpec((None, TB, 4), lambda h, s: (h, _bwd_block(s), 0))
    ghtf = pl.BlockSpec((None, 4, TB), lambda h, s: (h, 0, s))
    ghtb = pl.BlockSpec((None, 4, TB), lambda h, s: (h, 0, _bwd_block(s)))
    lat = _lat_seq_of_block
    ctx = lambda s: jnp.minimum(s, NB_CTX - 1)
    return pl.pallas_call(
        _mlstm_kernel,
        grid=(M_HEADS, NB),
        in_specs=[
            tokf, tokf, tokf, ghf, ghtf, tokb, tokb, tokb, ghb, ghtb,
            pl.BlockSpec((None, None, N_DIR, None, M_DH, M_DH), lambda h, s: (lat(s), 0, 0, h, 0, 0)),
            pl.BlockSpec((None, N_DIR, None, 1, M_DH), lambda h, s: (lat(s), 0, h, 0, 0)),
            pl.BlockSpec((None, N_DIR, None, 1, 1), lambda h, s: (lat(s), 0, h, 0, 0)),
        ],
        out_specs=[
            tokf, tokb,
            pl.BlockSpec((None, None, N_DIR, None, M_DH, M_DH), lambda h, s: (ctx(s), 0, 0, h, 0, 0)),
            pl.BlockSpec((None, N_DIR, None, 1, M_DH), lambda h, s: (ctx(s), 0, h, 0, 0)),
            pl.BlockSpec((None, N_DIR, None, 1, 1), lambda h, s: (ctx(s), 0, h, 0, 0)),
        ],
        out_shape=[
            jax.ShapeDtypeStruct((N_TOK, M_INNER), F32),
            jax.ShapeDtypeStruct((N_TOK, M_INNER), F32),
            jax.ShapeDtypeStruct((BATCH, 1, N_DIR, M_HEADS, M_DH, M_DH), F32),
            jax.ShapeDtypeStruct((BATCH, N_DIR, M_HEADS, 1, M_DH), F32),
            jax.ShapeDtypeStruct((BATCH, N_DIR, M_HEADS, 1, 1), F32),
        ],
        scratch_shapes=[
            pltpu.VMEM((N_DIR, M_DH, M_DH), F32),
            pltpu.VMEM((N_DIR, 1, M_DH), F32),
            pltpu.VMEM((N_DIR, 1, 1), F32),
        ],
        compiler_params=_cparams(("arbitrary", "arbitrary")),
        name="mlstm_scan",
    )(q, k, v, gh, ght, q, k, v, gh, ght, c0, n0, m0)


def _post_ln(x, branch, gate, ln_g, ln_b):
    return _layer_norm(ALPHA * x + gate * branch, ln_g, ln_b)


def _l0_out_kernel(hf_ref, hb_ref, xc_ref, z_ref, nw_ref, sk_ref, w_ref, xa_ref, mod_ref, lg_ref, lb_ref,
                   x1_ref, h2_ref, o_scr):
    for h in range(M_HEADS):
        c0, c1 = h * M_DH, (h + 1) * M_DH
        hs = hf_ref[:, c0:c1] + hb_ref[:, c0:c1]
        mu = jnp.mean(hs, axis=-1, keepdims=True)
        d = hs - mu
        var = jnp.mean(d * d, axis=-1, keepdims=True)
        hn = d * lax.rsqrt(var + LN_EPS) * nw_ref[:, c0:c1]
        o = (hn + sk_ref[:, c0:c1] * xc_ref[:, c0:c1]) * _silu(z_ref[:, c0:c1])
        o_scr[:, c0:c1] = o.astype(BF16)
    out = jnp.dot(o_scr[...], w_ref[...], preferred_element_type=F32)
    x1 = _post_ln(xa_ref[...], out, mod_ref[2:3, :], lg_ref[...], lb_ref[...])
    x1_ref[...] = x1
    h2_ref[...] = (x1 * (1.0 + mod_ref[4:5, :]) + mod_ref[3:4, :]).astype(BF16)


def _l0_out(hf, hb, xc, z, norm_w, skip, w_down, xa, mod, ln_g, ln_b):
    full = lambda shape: pl.BlockSpec(shape, lambda i: (0,) * len(shape))
    tok = lambda w: pl.BlockSpec((TB, w), lambda i: (i, 0))
    return pl.pallas_call(
        _l0_out_kernel,
        grid=(NB,),
        in_specs=[tok(M_INNER), tok(M_INNER), tok(M_INNER), tok(M_INNER),
                  full((1, M_INNER)), full((1, M_INNER)), full((M_INNER, D_MODEL)),
                  tok(D_MODEL), _mod_spec(0, TB), full((1, D_MODEL)), full((1, D_MODEL))],
        out_specs=[tok(D_MODEL), tok(D_MODEL)],
        out_shape=[jax.ShapeDtypeStruct((N_TOK, D_MODEL), F32),
                   jax.ShapeDtypeStruct((N_TOK, D_MODEL), BF16)],
        scratch_shapes=[pltpu.VMEM((TB, M_INNER), BF16)],
        compiler_params=_cparams(("arbitrary",)),
        name="l0_out",
    )(hf, hb, xc, z, norm_w, skip, w_down, xa, mod, ln_g, ln_b)


def _ffn_kernel(h_ref, w1_ref, w3_ref, w2_ref, x_ref, mod_ref, lg_ref, lb_ref, modn_ref,
                x2_ref, h3_ref, acc_ref):
    f = pl.program_id(1)

    @pl.when(f == 0)
    def _():
        acc_ref[...] = jnp.zeros_like(acc_ref)

    h = h_ref[...]
    a = jnp.dot(h, w1_ref[...], preferred_element_type=F32)
    b = jnp.dot(h, w3_ref[...], preferred_element_type=F32)
    acc_ref[...] += _bdot(_silu(a) * b, w2_ref[...])

    @pl.when(f == pl.num_programs(1) - 1)
    def _():
        x2 = _post_ln(x_ref[...], acc_ref[...], mod_ref[5:6, :], lg_ref[...], lb_ref[...])
        x2_ref[...] = x2
        h3_ref[...] = (x2 * (1.0 + modn_ref[1:2, :]) + modn_ref[0:1, :]).astype(BF16)


def _ffn(h2, w1, w3, w2, x1, mod, ln_g, ln_b):
    tok = lambda w: pl.BlockSpec((TM_FF, w), lambda i, f: (i, 0))
    vec = pl.BlockSpec((1, D_MODEL), lambda i, f: (0, 0))
    return pl.pallas_call(
        _ffn_kernel,
        grid=(N_TOK // TM_FF, D_FF // TF_FF),
        in_specs=[tok(D_MODEL),
                  pl.BlockSpec((D_MODEL, TF_FF), lambda i, f: (0, f)),
                  pl.BlockSpec((D_MODEL, TF_FF), lambda i, f: (0, f)),
                  pl.BlockSpec((TF_FF, D_MODEL), lambda i, f: (f, 0)),
                  tok(D_MODEL), _mod_spec(0, TM_FF), vec, vec, _mod_spec(1, TM_FF)],
        out_specs=[tok(D_MODEL), tok(D_MODEL)],
        out_shape=[jax.ShapeDtypeStruct((N_TOK, D_MODEL), F32),
                   jax.ShapeDtypeStruct((N_TOK, D_MODEL), BF16)],
        scratch_shapes=[pltpu.VMEM((TM_FF, D_MODEL), F32)],
        compiler_params=_cparams(("arbitrary", "arbitrary")),
        name="ffn",
    )(h2, w1, w3, w2, x1, mod, ln_g, ln_b, mod)


def _l1_in_kernel(h_ref, wz_ref, wx_ref, wd_ref, z_ref, xbc_ref, dt_ref):
    h = h_ref[...]
    z_ref[...] = jnp.dot(h, wz_ref[...], preferred_element_type=F32)
    xbc_ref[...] = jnp.dot(h, wx_ref[...], preferred_element_type=F32)
    dt_ref[...] = jnp.dot(h, wd_ref[...], preferred_element_type=F32)


def _l1_in(h3, wz, wx, wd):
    full = lambda shape: pl.BlockSpec(shape, lambda i: (0,) * len(shape))
    tok = lambda w: pl.BlockSpec((TB, w), lambda i: (i, 0))
    return pl.pallas_call(
        _l1_in_kernel,
        grid=(NB,),
        in_specs=[tok(D_MODEL), full((D_MODEL, S_INNER)), full((D_MODEL, S_CONV_CH)), full((D_MODEL, LANES))],
        out_specs=[tok(S_INNER), tok(S_CONV_CH), tok(LANES)],
        out_shape=[jax.ShapeDtypeStruct((N_TOK, S_INNER), F32),
                   jax.ShapeDtypeStruct((N_TOK, S_CONV_CH), F32),
                   jax.ShapeDtypeStruct((N_TOK, LANES), F32)],
        compiler_params=_cparams(("arbitrary",)),
        name="l1_in",
    )(h3, wz, wx, wd)


def _l1_conv_kernel(prev_ref, xbc_ref, next_ref, cw_ref, cb_ref, dtr_ref, dtb_ref, a_ref,
                    xs_ref, bmt_ref, cm_ref, dt_ref, cum_ref, ext_ref):
    _fill_ext(ext_ref, prev_ref, xbc_ref, next_ref)
    cw = 512
    for j in range(S_INNER // cw):
        c0, c1 = j * cw, (j + 1) * cw
        xs_ref[:, c0:c1] = _silu(_conv_cols(ext_ref, cw_ref, cb_ref, c0, c1))
    for j in range(S_GN // cw):
        c0, c1 = S_INNER + j * cw, S_INNER + (j + 1) * cw
        bm = _silu(_conv_cols(ext_ref, cw_ref, cb_ref, c0, c1))
        bmt_ref[j * cw:(j + 1) * cw, :] = bm.T.astype(BF16)
        c0, c1 = c0 + S_GN, c1 + S_GN
        cm_ref[:, j * cw:(j + 1) * cw] = _silu(_conv_cols(ext_ref, cw_ref, cb_ref, c0, c1)).astype(BF16)
    x = dtr_ref[...] + dtb_ref[...]
    dt = jnp.maximum(x, 0.0) + jnp.log1p(jnp.exp(-jnp.abs(x)))
    dt_ref[...] = dt
    da = dt * a_ref[...]
    lower, upper = _tri_masks()
    lane_c = lax.broadcasted_iota(jnp.int32, (CHUNK, LANES), 1)
    for c in range(CPB):
        dc = da[c * CHUNK:(c + 1) * CHUNK, :]
        pref = _tri_dot(lower.astype(BF16), dc)
        suff = _tri_dot(upper.astype(BF16), dc)
        cum_ref[c * CHUNK:(c + 1) * CHUNK, :] = jnp.where(lane_c < S_HEADS, pref, suff)


def _l1_conv(xbc, conv_w, conv_b, dt_raw, dt_bias, a_neg):
    full = lambda shape: pl.BlockSpec(shape, lambda i: (0,) * len(shape))
    tok = lambda w: pl.BlockSpec((TB, w), lambda i: (i, 0))
    return pl.pallas_call(
        _l1_conv_kernel,
        grid=(NB,),
        in_specs=_halo_specs(S_CONV_CH) + [
            full((CONV_W, S_CONV_CH)), full((1, S_CONV_CH)), tok(LANES), full((1, LANES)), full((1, LANES))],
        out_specs=[tok(S_INNER), pl.BlockSpec((S_GN, TB), lambda i: (0, i)), tok(S_GN), tok(LANES), tok(LANES)],
        out_shape=[jax.ShapeDtypeStruct((N_TOK, S_INNER), F32),
                   jax.ShapeDtypeStruct((S_GN, N_TOK), BF16),
                   jax.ShapeDtypeStruct((N_TOK, S_GN), BF16),
                   jax.ShapeDtypeStruct((N_TOK, LANES), F32),
                   jax.ShapeDtypeStruct((N_TOK, LANES), F32)],
        scratch_shapes=[pltpu.VMEM((TB + 2 * HALO, S_CONV_CH), F32)],
        compiler_params=_cparams(("arbitrary",)),
        name="l1_conv",
    )(xbc, xbc, xbc, conv_w, conv_b, dt_raw, dt_bias, a_neg)


GW = S_HPG * S_HEADDIM


def _ssd_chunk(r, x, bmt, cm, dg, ct, dsk, st_ref, xw_ref, y_ref, rows):
    lower, upper = _tri_masks()
    mask = lower if r == 0 else upper
    st = st_ref[r]
    cb = jnp.dot(cm, bmt, preferred_element_type=F32)
    y2 = _bdot(cm, st)
    lane = lax.broadcasted_iota(jnp.int32, (1, GW), 1)
    dec = jnp.zeros((1, GW), F32)
    for j in range(S_HPG):
        c = S_HPG * r + j
        p0, p1 = j * S_HEADDIM, (j + 1) * S_HEADDIM
        dt_col = dg[:, c:c + 1]
        cum_col = dg[:, 2 * S_HPG + c:2 * S_HPG + c + 1]
        cum_row = ct[c:c + 1, :]
        decay = jnp.exp(jnp.where(mask, cum_col - cum_row, -jnp.inf))
        xj = x[:, p0:p1]
        y1 = _bdot(decay * cb, xj * dt_col)
        yj = y1 + jnp.exp(cum_col) * y2[:, p0:p1]
        if r == 0:
            yj = yj + dsk[:, p0:p1] * xj
        y_ref[rows, p0:p1] = yj
        last = cum_col[CHUNK - 1:CHUNK, :] if r == 0 else cum_col[0:1, :]
        xw_ref[:, p0:p1] = xj * (jnp.exp(last - cum_col) * dt_col)
        dec = jnp.where(jnp.logical_and(lane >= p0, lane < p1), jnp.exp(last), dec)
    st_ref[r] = dec * st + _bdot(bmt, xw_ref[...])


def _ssd_kernel(xf_ref, bf_ref, cf_ref, dgf_ref, ctf_ref, xb_ref, bb_ref, cb_ref, dgb_ref, ctb_ref,
                dsk_ref, s0_ref, yf_ref, yb_ref, so_ref, st_ref, xw_ref):
    s = pl.program_id(1)

    @pl.when(_seq_start(s))
    def _():
        is_ctx = s < NB_CTX
        for r in range(N_DIR):
            s0 = jnp.concatenate([s0_ref[r, j] for j in range(S_HPG)], axis=0)
            st_ref[r] = jnp.where(is_ctx, 0.0, s0.T)

    dsk = dsk_ref[...]
    for c in range(CPB):
        rows = slice(c * CHUNK, (c + 1) * CHUNK)
        _ssd_chunk(0, xf_ref[rows, :], bf_ref[:, rows], cf_ref[rows, :], dgf_ref[rows, :], ctf_ref[:, rows],
                   dsk, st_ref, xw_ref, yf_ref, rows)
    for c in reversed(range(CPB)):
        rows = slice(c * CHUNK, (c + 1) * CHUNK)
        _ssd_chunk(1, xb_ref[rows, :], bb_ref[:, rows], cb_ref[rows, :], dgb_ref[rows, :], ctb_ref[:, rows],
                   dsk, st_ref, xw_ref, yb_ref, rows)

    @pl.when(s < NB_CTX)
    def _():
        for r in range(N_DIR):
            stt = st_ref[r].T
            for j in range(S_HPG):
                so_ref[r, j] = stt[j * S_HEADDIM:(j + 1) * S_HEADDIM, :]


def _ssd_scan(xs, bmt, cm, dg, ct, dsk, s0):
    ng = 2 * N_DIR * S_HPG
    def side(blk):
        return [
            pl.BlockSpec((TB, GW), lambda g, s: (blk(s), g)),
            pl.BlockSpec((S_STATE, TB), lambda g, s: (g, blk(s))),
            pl.BlockSpec((TB, S_STATE), lambda g, s: (blk(s), g)),
            pl.BlockSpec((None, TB, ng), lambda g, s: (g, blk(s), 0)),
            pl.BlockSpec((None, N_DIR * S_HPG, TB), lambda g, s: (g, 0, blk(s))),
        ]
    ident = lambda s: s
    lat = _lat_seq_of_block
    ctx = lambda s: jnp.minimum(s, NB_CTX - 1)
    st_block = (None, None, N_DIR, S_HPG, S_HEADDIM, S_STATE)
    return pl.pallas_call(
        _ssd_kernel,
        grid=(S_GROUPS, NB),
        in_specs=side(ident) + side(_bwd_block) + [
            pl.BlockSpec((None, 1, GW), lambda g, s: (g, 0, 0)),
            pl.BlockSpec(st_block, lambda g, s: (lat(s), 0, 0, g, 0, 0)),
        ],
        out_specs=[
            pl.BlockSpec((TB, GW), lambda g, s: (s, g)),
            pl.BlockSpec((TB, GW), lambda g, s: (_bwd_block(s), g)),
            pl.BlockSpec(st_block, lambda g, s: (ctx(s), 0, 0, g, 0, 0)),
        ],
        out_shape=[
            jax.ShapeDtypeStruct((N_TOK, S_INNER), F32),
            jax.ShapeDtypeStruct((N_TOK, S_INNER), F32),
            jax.ShapeDtypeStruct((BATCH, 1, N_DIR, S_HEADS, S_HEADDIM, S_STATE), F32),
        ],
        scratch_shapes=[pltpu.VMEM((N_DIR, S_STATE, GW), F32), pltpu.VMEM((CHUNK, GW), F32)],
        compiler_params=_cparams(("arbitrary", "arbitrary")),
        name="ssd_scan",
    )(xs, bmt, cm, dg, ct, xs, bmt, cm, dg, ct, dsk, s0)


def _l1_out_kernel(yf_ref, yb_ref, z_ref, nw_ref, w_ref, x_ref, mod_ref, lg_ref, lb_ref, wr_ref,
                   x3_ref, h4_ref, gates_ref):
    y = (yf_ref[...] + yb_ref[...]) * _silu(z_ref[...])
    y = y * lax.rsqrt(jnp.mean(y * y, axis=-1, keepdims=True) + RMS_EPS) * nw_ref[...]
    out = _bdot(y, w_ref[...])
    x3 = _post_ln(x_ref[...], out, mod_ref[2:3, :], lg_ref[...], lb_ref[...])
    x3_ref[...] = x3
    h4 = x3 * (1.0 + mod_ref[4:5, :]) + mod_ref[3:4, :]
    h4_ref[...] = h4
    lane = lax.broadcasted_iota(jnp.int32, (TB, LANES), 1)
    logits = jnp.full((TB, LANES), -jnp.inf, F32)
    for e in range(N_EXPERTS):
        le = jnp.sum(h4 * wr_ref[e:e + 1, :], axis=-1, keepdims=True)
        logits = jnp.where(lane == e, le, logits)
    v1 = jnp.max(logits, axis=-1, keepdims=True)
    i1 = jnp.min(jnp.where(logits == v1, lane, LANES), axis=-1, keepdims=True)
    rest = jnp.where(lane == i1, -jnp.inf, logits)
    v2 = jnp.max(rest, axis=-1, keepdims=True)
    i2 = jnp.min(jnp.where(rest == v2, lane, LANES), axis=-1, keepdims=True)
    e2 = jnp.exp(v2 - v1)
    w1 = 1.0 / (1.0 + e2)
    w2 = e2 / (1.0 + e2)
    route = jnp.where(lane == 0, w1, jnp.where(lane == 1, w2, 0.0))
    route = jnp.where(lane == 2, i1.astype(F32), jnp.where(lane == 3, i2.astype(F32), route))
    gates_ref[...] = route


def _l1_out(yf, yb, z, norm_w, w_out, x2, mod, ln_g, ln_b, wr_t):
    full = lambda shape: pl.BlockSpec(shape, lambda i: (0,) * len(shape))
    tok = lambda w: pl.BlockSpec((TB, w), lambda i: (i, 0))
    return pl.pallas_call(
        _l1_out_kernel,
        grid=(NB,),
        in_specs=[tok(S_INNER), tok(S_INNER), tok(S_INNER), full((1, S_INNER)), full((S_INNER, D_MODEL)),
                  tok(D_MODEL), _mod_spec(1, TB), full((1, D_MODEL)), full((1, D_MODEL)),
                  full((N_EXPERTS, D_MODEL))],
        out_specs=[tok(D_MODEL), tok(D_MODEL), tok(LANES)],
        out_shape=[jax.ShapeDtypeStruct((N_TOK, D_MODEL), F32),
                   jax.ShapeDtypeStruct((N_TOK, D_MODEL), F32),
                   jax.ShapeDtypeStruct((N_TOK, LANES), F32)],
        compiler_params=_cparams(("arbitrary",)),
        name="l1_out",
    )(yf, yb, z, norm_w, w_out, x2, mod, ln_g, ln_b, wr_t)


TM_E = 512
N_ASSIGN = TOP_K * N_TOK
P_ROWS = N_ASSIGN + N_EXPERTS * TM_E
N_PT = P_ROWS // TM_E
GR = 1024


def _row_gather_kernel(idx_ref, src_ref, dst_ref, sem):
    base = pl.program_id(0) * GR

    def issue(r, carry):
        pltpu.make_async_copy(src_ref.at[pl.ds(idx_ref[0, r], 1)], dst_ref.at[pl.ds(base + r, 1)], sem).start()
        return carry

    lax.fori_loop(0, GR, issue, 0, unroll=8)
    pltpu.make_async_copy(src_ref.at[pl.ds(0, GR)], dst_ref.at[pl.ds(base, GR)], sem).wait()


def _row_gather(idx, src, name):
    n_out = idx.shape[0]
    return pl.pallas_call(
        _row_gather_kernel,
        grid=(n_out // GR,),
        in_specs=[pl.BlockSpec((None, 1, GR), lambda j: (j, 0, 0), memory_space=pltpu.SMEM),
                  pl.BlockSpec(memory_space=pl.ANY)],
        out_specs=pl.BlockSpec(memory_space=pl.ANY),
        out_shape=jax.ShapeDtypeStruct((n_out, src.shape[1]), src.dtype),
        scratch_shapes=[pltpu.SemaphoreType.DMA(())],
        compiler_params=_cparams(("arbitrary",)),
        name=name,
    )(idx.reshape(n_out // GR, 1, GR), src)


def _expert_kernel(te_ref, nv_ref, x_ref, w1_ref, w3_ref, w2_ref, y_ref, xb_ref):
    j, f = pl.program_id(0), pl.program_id(1)
    valid = j < nv_ref[0]

    @pl.when(f == 0)
    def _():
        xb_ref[...] = x_ref[...].astype(BF16)
        y_ref[...] = jnp.zeros_like(y_ref)

    @pl.when(valid)
    def _():
        x = xb_ref[...]
        a = jnp.dot(x, w1_ref[...], preferred_element_type=F32)
        b = jnp.dot(x, w3_ref[...], preferred_element_type=F32)
        y_ref[...] += _bdot(_silu(a) * b, w2_ref[...])


def _experts(tile_expert, n_valid, xs, w1, w3, w2):
    nf = E_FF // TF_MOE
    fidx = lambda j, f, nv: jnp.where(j < nv[0], f, nf - 1)
    return pl.pallas_call(
        _expert_kernel,
        grid_spec=pltpu.PrefetchScalarGridSpec(
            num_scalar_prefetch=2,
            grid=(N_PT, nf),
            in_specs=[pl.BlockSpec((TM_E, D_MODEL), lambda j, f, te, nv: (j, 0)),
                      pl.BlockSpec((None, D_MODEL, TF_MOE), lambda j, f, te, nv: (te[j], 0, fidx(j, f, nv))),
                      pl.BlockSpec((None, D_MODEL, TF_MOE), lambda j, f, te, nv: (te[j], 0, fidx(j, f, nv))),
                      pl.BlockSpec((None, TF_MOE, D_MODEL), lambda j, f, te, nv: (te[j], fidx(j, f, nv), 0))],
            out_specs=pl.BlockSpec((TM_E, D_MODEL), lambda j, f, te, nv: (j, 0)),
            scratch_shapes=[pltpu.VMEM((TM_E, D_MODEL), BF16)],
        ),
        out_shape=jax.ShapeDtypeStruct((P_ROWS, D_MODEL), F32),
        compiler_params=_cparams(("arbitrary", "arbitrary")),
        name="experts",
    )(tile_expert, n_valid, xs, w1, w3, w2)


def _moe_out_kernel(y0_ref, y1_ref, r_ref, x_ref, mod_ref, lg_ref, lb_ref, yp_ref, ys_ref):
    i = pl.program_id(0)
    f = r_ref[:, 0:1] * y0_ref[...] + r_ref[:, 1:2] * y1_ref[...]
    y = _post_ln(x_ref[...], f, mod_ref[5:6, :], lg_ref[...], lb_ref[...])

    @pl.when(i < N_CTX // TM_FF)
    def _():
        yp_ref[...] = y

    @pl.when(i >= N_CTX // TM_FF)
    def _():
        ys_ref[...] = y


def _moe_out(y0, y1, route, x3, mod, ln_g, ln_b):
    nctx = N_CTX // TM_FF
    tok = lambda w: pl.BlockSpec((TM_FF, w), lambda i: (i, 0))
    vec = pl.BlockSpec((1, D_MODEL), lambda i: (0, 0))
    return pl.pallas_call(
        _moe_out_kernel,
        grid=(N_TOK // TM_FF,),
        in_specs=[tok(D_MODEL), tok(D_MODEL), tok(LANES), tok(D_MODEL), _mod_spec(1, TM_FF), vec, vec],
        out_specs=[pl.BlockSpec((TM_FF, D_MODEL), lambda i: (jnp.minimum(i, nctx - 1), 0)),
                   pl.BlockSpec((TM_FF, D_MODEL), lambda i: (jnp.maximum(i - nctx, 0), 0))],
        out_shape=[jax.ShapeDtypeStruct((N_CTX, D_MODEL), F32),
                   jax.ShapeDtypeStruct((N_LAT, D_MODEL), F32)],
        compiler_params=_cparams(("arbitrary",)),
        name="moe_out",
    )(y0, y1, route, x3, mod, ln_g, ln_b)


def _routing_tables(route):
    flat_e = route[:, 2:2 + TOP_K].astype(jnp.int32).reshape(N_ASSIGN)
    onehot = (flat_e[:, None] == jnp.arange(N_EXPERTS, dtype=jnp.int32)[None, :]).astype(jnp.int32)
    csum = jnp.cumsum(onehot, axis=0)
    rank = jnp.take_along_axis(csum, flat_e[:, None], axis=1)[:, 0] - 1
    ntile = (csum[-1] + TM_E - 1) // TM_E
    tile_end = jnp.cumsum(ntile)
    dest = (tile_end - ntile)[flat_e] * TM_E + rank
    src = jnp.zeros((P_ROWS,), jnp.int32).at[dest].set(jnp.arange(N_ASSIGN, dtype=jnp.int32) // TOP_K)
    tiles = jnp.arange(N_PT, dtype=jnp.int32)
    n_valid = tile_end[-1]
    tile_expert = jnp.sum((jnp.minimum(tiles, n_valid - 1)[:, None] >= tile_end[None, :]).astype(jnp.int32), axis=1)
    dest = dest.reshape(N_TOK, TOP_K)
    return src, dest[:, 0], dest[:, 1], tile_expert, n_valid.reshape(1)


def _grid_pos_embed():
    rows = DEC_SEQ // GRID_W
    quarter = D_MODEL // 4
    freqs = jnp.exp(-math.log(POS_BASE) * jnp.arange(quarter, dtype=F32) / quarter)
    er = jnp.arange(rows, dtype=F32)[:, None] * freqs
    ec = jnp.arange(GRID_W, dtype=F32)[:, None] * freqs
    row_emb = jnp.concatenate([jnp.sin(er), jnp.cos(er)], axis=-1)
    col_emb = jnp.concatenate([jnp.sin(ec), jnp.cos(ec)], axis=-1)
    pos = jnp.concatenate([jnp.broadcast_to(row_emb[:, None, :], (rows, GRID_W, D_MODEL // 2)),
                           jnp.broadcast_to(col_emb[None, :, :], (rows, GRID_W, D_MODEL // 2))], axis=-1)
    return pos.reshape(DEC_SEQ, D_MODEL)


def _pad_lanes(a, width=LANES):
    return jnp.pad(a, [(0, 0)] * (a.ndim - 1) + [(0, width - a.shape[-1])])


def kernel(x_prompt, x_sample, c, state_mlstm_C, state_mlstm_n, state_mlstm_m, state_ssm, c_ctx,
           ada_w, ada_b, ln_g, ln_b,
           ml_w_up, ml_conv_w, ml_conv_b, ml_wq, ml_wk, ml_wv, ml_w_ig, ml_b_ig, ml_w_fg, ml_b_fg,
           ml_norm_w, ml_skip, ml_w_down,
           ss_w_in, ss_conv_w, ss_conv_b, ss_dt_bias, ss_a_log, ss_d, ss_norm_w, ss_w_out,
           ff_w1, ff_w3, ff_w2,
           moe_w_router, moe_w1, moe_w3, moe_w2):
    row = lambda a: a.reshape(1, -1)
    cond = jnp.concatenate([c_ctx[None, :], c, jnp.zeros((COND_ROWS - N_COND, D_MODEL), F32)], axis=0)
    mod = _ada_mod(cond, ada_w.astype(BF16), ada_b)

    xa, xm, z0 = _l0_in(x_prompt.reshape(N_CTX, D_MODEL), x_sample.reshape(N_LAT, D_MODEL), _grid_pos_embed(),
                        mod, ml_w_up[0].astype(BF16))
    ng = N_DIR * M_HEADS
    wg = jnp.concatenate([ml_w_ig[0].transpose(1, 0, 2).reshape(3 * M_INNER, ng),
                          ml_w_fg[0].transpose(1, 0, 2).reshape(3 * M_INNER, ng)], axis=1)
    bg = jnp.concatenate([ml_b_ig[0].reshape(ng), ml_b_fg[0].reshape(ng)])
    xc, q, k, v, g, bc = _l0_qkv(xm, ml_conv_w[0], row(ml_conv_b[0]),
                                 ml_wq[0].astype(BF16), ml_wk[0].astype(BF16), ml_wv[0].astype(BF16),
                                 _pad_lanes(wg).astype(BF16), row(_pad_lanes(bg)))
    gh = jnp.concatenate([g[:, :ng].reshape(N_TOK, N_DIR, M_HEADS),
                          bc[:, ng:2 * ng].reshape(N_TOK, N_DIR, M_HEADS)], axis=1)
    ght = gh.transpose(2, 1, 0)
    gh = gh.transpose(2, 0, 1)
    hf, hb, new_c, new_n, new_m = _mlstm_scan(
        q, k, v, gh, ght, state_mlstm_C,
        state_mlstm_n.reshape(DEC_BATCH, N_DIR, M_HEADS, 1, M_DH),
        state_mlstm_m.reshape(DEC_BATCH, N_DIR, M_HEADS, 1, 1))
    x1, h2 = _l0_out(hf, hb, xc, z0, row(ml_norm_w[0]), row(ml_skip[0]), ml_w_down[0].astype(BF16),
                     xa, mod, row(ln_g[0, 0]), row(ln_b[0, 0]))
    x2, h3 = _ffn(h2, ff_w1[0].astype(BF16), ff_w3[0].astype(BF16), ff_w2[0].astype(BF16),
                  x1, mod, row(ln_g[0, 1]), row(ln_b[0, 1]))

    w_in = ss_w_in[0]
    z1, xbc, dt_raw = _l1_in(h3, w_in[:, :S_INNER].astype(BF16),
                             w_in[:, S_INNER:S_INNER + S_CONV_CH].astype(BF16),
                             _pad_lanes(w_in[:, S_INNER + S_CONV_CH:]).astype(BF16))
    a_neg = -jnp.exp(ss_a_log[0].reshape(N_DIR * S_HEADS))
    xs, bmt, cm, dt, cum = _l1_conv(xbc, ss_conv_w[0], row(ss_conv_b[0]), dt_raw,
                                    row(_pad_lanes(ss_dt_bias[0].reshape(N_DIR * S_HEADS))), row(_pad_lanes(a_neg)))
    nh = N_DIR * S_HEADS
    to_group = lambda a: a[:, :nh].reshape(N_TOK, N_DIR, S_GROUPS, S_HPG).transpose(2, 0, 1, 3).reshape(
        S_GROUPS, N_TOK, N_DIR * S_HPG)
    cum_g = to_group(cum)
    dg = jnp.concatenate([to_group(dt), cum_g], axis=-1)
    ct = cum_g.transpose(0, 2, 1)
    dsk = jnp.repeat(ss_d[0], S_HEADDIM).reshape(S_GROUPS, 1, GW)
    yf, yb, new_s = _ssd_scan(xs, bmt, cm, dg, ct, dsk, state_ssm)
    x3, h4, route = _l1_out(yf, yb, z1, row(ss_norm_w[0]), ss_w_out[0].astype(BF16), x2, mod,
                            row(ln_g[1, 0]), row(ln_b[1, 0]), moe_w_router[0].T)
    src, dest0, dest1, tile_expert, n_valid = _routing_tables(route)
    xs_rows = _row_gather(src, h4, "moe_dispatch")
    y_rows = _experts(tile_expert, n_valid, xs_rows,
                      moe_w1[0].astype(BF16), moe_w3[0].astype(BF16), moe_w2[0].astype(BF16))
    y0 = _row_gather(dest0, y_rows, "moe_combine0")
    y1 = _row_gather(dest1, y_rows, "moe_combine1")
    yp, ys = _moe_out(y0, y1, route, x3, mod, row(ln_g[1, 1]), row(ln_b[1, 1]))

    return (yp.reshape(BATCH, SEQ, D_MODEL), ys.reshape(DEC_BATCH, DEC_SEQ, D_MODEL),
            new_c,
            new_n.reshape(BATCH, 1, N_DIR, M_HEADS, M_DH),
            new_m.reshape(BATCH, 1, N_DIR, M_HEADS),
            new_s)
```

```python
import functools
import math

import jax
import jax.numpy as jnp
import numpy as np
from jax import lax
from jax.experimental import pallas as pl
from jax.experimental.pallas import tpu as pltpu

F32 = jnp.float32
BF16 = jnp.bfloat16

D_MODEL = 1024
BATCH = 16
SEQ = 256
DEC_BATCH = 4
DEC_SEQ = 4096
GRID_W = 64
CHUNK = 128
CONV_W = 5
N_DIR = 2
M_INNER = 2 * D_MODEL
M_HEADS = 4
M_DH = M_INNER // M_HEADS
S_INNER = 2 * D_MODEL
S_HEADDIM = 64
S_HEADS = S_INNER // S_HEADDIM
S_GROUPS = 8
S_HPG = S_HEADS // S_GROUPS
S_STATE = 128
S_GN = S_GROUPS * S_STATE
S_CONV_CH = S_INNER + 2 * S_GN
D_FF = 2816
N_EXPERTS = 8
TOP_K = 2
E_FF = 3584
DEPTH = 2
ALPHA = (2 * DEPTH) ** 0.25
LN_EPS = 1e-5
RMS_EPS = 1e-5
POS_BASE = 10000.0

N_CTX = BATCH * SEQ
N_LAT = DEC_BATCH * DEC_SEQ
N_TOK = N_CTX + N_LAT
N_COND = 1 + DEC_BATCH
COND_ROWS = 8

TB = 256
NB = N_TOK // TB
NB_CTX = N_CTX // TB
NB_SEQ = DEC_SEQ // TB
HALO = 8
CPB = TB // CHUNK
MCH = TB

TM_FF = 512
TF_FF = 1408
TF_MOE = 1792
LANES = 128

VMEM_LIMIT = 56 * 1024 * 1024


def _cparams(sem):
    return pltpu.CompilerParams(dimension_semantics=sem, vmem_limit_bytes=VMEM_LIMIT)


def _cond_of_token(t0):
    return jnp.where(t0 < N_CTX, 0, 1 + (t0 - N_CTX) // DEC_SEQ)


def _lat_seq_of_block(s):
    return jnp.maximum(s - NB_CTX, 0) // NB_SEQ


def _bwd_block(s):
    k = s - NB_CTX
    return jnp.where(s < NB_CTX, s, NB_CTX + (k // NB_SEQ) * NB_SEQ + (NB_SEQ - 1 - k % NB_SEQ))


def _seq_start(s):
    return jnp.logical_or(s < NB_CTX, (s - NB_CTX) % NB_SEQ == 0)


def _seq_end(s):
    return jnp.logical_or(s < NB_CTX, (s - NB_CTX) % NB_SEQ == NB_SEQ - 1)


def _silu(x):
    return x * jax.nn.sigmoid(x)


def _bdot(a, b):
    return jnp.dot(a.astype(BF16), b.astype(BF16), preferred_element_type=F32)


def _tri_dot(tri, x):
    hi = x.astype(BF16)
    r1 = x - hi.astype(F32)
    mid = r1.astype(BF16)
    lo = (r1 - mid.astype(F32)).astype(BF16)
    d = lambda a: jnp.dot(tri, a, preferred_element_type=F32)
    return d(hi) + d(mid) + d(lo)


def _tri_masks(n=CHUNK):
    t = lax.broadcasted_iota(jnp.int32, (n, n), 0)
    u = lax.broadcasted_iota(jnp.int32, (n, n), 1)
    return u <= t, u >= t


def _layer_norm(r, g, b):
    mu = jnp.mean(r, axis=-1, keepdims=True)
    d = r - mu
    var = jnp.mean(d * d, axis=-1, keepdims=True)
    return d * lax.rsqrt(var + LN_EPS) * g + b


def _ada_kernel(cond_ref, w_ref, b_ref, o_ref):
    o_ref[...] = _bdot(_silu(cond_ref[...]), w_ref[...]) + b_ref[...]


def _ada_mod(cond, ada_w, ada_b):
    ncol = 6 * D_MODEL // D_MODEL
    out = pl.pallas_call(
        _ada_kernel,
        grid=(DEPTH, ncol),
        in_specs=[
            pl.BlockSpec((COND_ROWS, D_MODEL), lambda l, j: (0, 0)),
            pl.BlockSpec((None, D_MODEL, D_MODEL), lambda l, j: (l, 0, j)),
            pl.BlockSpec((None, 1, D_MODEL), lambda l, j: (l, 0, j)),
        ],
        out_specs=pl.BlockSpec((None, COND_ROWS, D_MODEL), lambda l, j: (l, 0, j)),
        out_shape=jax.ShapeDtypeStruct((DEPTH, COND_ROWS, 6 * D_MODEL), F32),
        compiler_params=_cparams(("arbitrary", "arbitrary")),
        name="ada_mod",
    )(cond, ada_w, ada_b.reshape(DEPTH, 1, 6 * D_MODEL))
    return out.reshape(DEPTH, COND_ROWS, 6, D_MODEL)


def _mod_spec(layer, tm):
    return pl.BlockSpec((None, None, 6, D_MODEL), lambda i, *_: (layer, _cond_of_token(i * tm), 0, 0))


def _l0_in_kernel(xp_ref, xs_ref, pos_ref, mod_ref, w_ref, xa_ref, xm_ref, z_ref):
    i = pl.program_id(0)
    x = jnp.where(i >= NB_CTX, xs_ref[...] + pos_ref[...], xp_ref[...])
    xa_ref[...] = x
    h = x * (1.0 + mod_ref[1:2, :]) + mod_ref[0:1, :]
    u = _bdot(h, w_ref[...])
    xm_ref[...] = u[:, :M_INNER]
    z_ref[...] = u[:, M_INNER:]


def _l0_in(xp, xs, pos, mod, w_up):
    return pl.pallas_call(
        _l0_in_kernel,
        grid=(NB,),
        in_specs=[
            pl.BlockSpec((TB, D_MODEL), lambda i: (jnp.minimum(i, NB_CTX - 1), 0)),
            pl.BlockSpec((TB, D_MODEL), lambda i: (jnp.maximum(i - NB_CTX, 0), 0)),
            pl.BlockSpec((TB, D_MODEL), lambda i: (jnp.maximum(i - NB_CTX, 0) % NB_SEQ, 0)),
            _mod_spec(0, TB),
            pl.BlockSpec((D_MODEL, 2 * M_INNER), lambda i: (0, 0)),
        ],
        out_specs=[
            pl.BlockSpec((TB, D_MODEL), lambda i: (i, 0)),
            pl.BlockSpec((TB, M_INNER), lambda i: (i, 0)),
            pl.BlockSpec((TB, M_INNER), lambda i: (i, 0)),
        ],
        out_shape=[
            jax.ShapeDtypeStruct((N_TOK, D_MODEL), F32),
            jax.ShapeDtypeStruct((N_TOK, M_INNER), F32),
            jax.ShapeDtypeStruct((N_TOK, M_INNER), F32),
        ],
        compiler_params=_cparams(("arbitrary",)),
        name="l0_in",
    )(xp, xs, pos, mod, w_up)


def _fill_ext(ext_ref, prev_ref, main_ref, next_ref):
    i = pl.program_id(0)
    ext_ref[0:HALO, :] = jnp.where(_seq_start(i), 0.0, prev_ref[...])
    ext_ref[HALO:HALO + TB, :] = main_ref[...]
    ext_ref[HALO + TB:HALO + TB + HALO, :] = jnp.where(_seq_end(i), 0.0, next_ref[...])


def _conv_cols(ext_ref, w_ref, b_ref, c0, c1):
    acc = b_ref[:, c0:c1]
    for j in range(CONV_W):
        r0 = HALO - CONV_W // 2 + j
        acc = acc + w_ref[j:j + 1, c0:c1] * ext_ref[r0:r0 + TB, c0:c1]
    return acc


def _halo_specs(width):
    nh = TB // HALO
    last = N_TOK // HALO - 1
    return [
        pl.BlockSpec((HALO, width), lambda i: (jnp.maximum(i * nh - 1, 0), 0)),
        pl.BlockSpec((TB, width), lambda i: (i, 0)),
        pl.BlockSpec((HALO, width), lambda i: (jnp.minimum((i + 1) * nh, last), 0)),
    ]


def _l0_qkv_kernel(prev_ref, xm_ref, next_ref, cw_ref, cb_ref, wq_ref, wk_ref, wv_ref, wg_ref, bg_ref,
                   xc_ref, q_ref, k_ref, v_ref, g_ref, bc_ref, ext_ref):
    _fill_ext(ext_ref, prev_ref, xm_ref, next_ref)
    pre = jnp.zeros((TB, LANES), F32) + bg_ref[...]
    for h in range(M_HEADS):
        c0, c1 = h * M_DH, (h + 1) * M_DH
        xc = _silu(_conv_cols(ext_ref, cw_ref, cb_ref, c0, c1))
        xc_ref[:, c0:c1] = xc
        xcb = xc.astype(BF16)
        q = jnp.dot(xcb, wq_ref[h], preferred_element_type=F32).astype(BF16)
        k = (jnp.dot(xcb, wk_ref[h], preferred_element_type=F32) * (M_DH ** -0.5)).astype(BF16)
        v = jnp.dot(xm_ref[:, c0:c1].astype(BF16), wv_ref[h], preferred_element_type=F32).astype(BF16)
        q_ref[:, c0:c1] = q
        k_ref[:, c0:c1] = k
        v_ref[:, c0:c1] = v
        pre = pre + jnp.dot(q, wg_ref[c0:c1, :], preferred_element_type=F32)
        pre = pre + jnp.dot(k, wg_ref[M_INNER + c0:M_INNER + c1, :], preferred_element_type=F32)
        pre = pre + jnp.dot(v, wg_ref[2 * M_INNER + c0:2 * M_INNER + c1, :], preferred_element_type=F32)
    lane = lax.broadcasted_iota(jnp.int32, (TB, LANES), 1)
    ng = N_DIR * M_HEADS
    logsig = -(jnp.maximum(-pre, 0.0) + jnp.log1p(jnp.exp(-jnp.abs(pre))))
    g = jnp.where(jnp.logical_and(lane >= ng, lane < 2 * ng), logsig, pre)
    g_ref[...] = g
    lower, upper = _tri_masks(MCH)
    lane_c = lax.broadcasted_iota(jnp.int32, (MCH, LANES), 1)
    fwd_lane = jnp.logical_and(lane_c >= ng, lane_c < ng + M_HEADS)
    for c in range(TB // MCH):
        gc = g[c * MCH:(c + 1) * MCH, :]
        pref = _tri_dot(lower.astype(BF16), gc)
        suff = _tri_dot(upper.astype(BF16), gc)
        bc_ref[c * MCH:(c + 1) * MCH, :] = jnp.where(fwd_lane, pref, suff)


def _l0_qkv(xm, conv_w, conv_b, wq, wk, wv, wg, bg):
    full = lambda shape: pl.BlockSpec(shape, lambda i: (0,) * len(shape))
    tok = lambda w: pl.BlockSpec((TB, w), lambda i: (i, 0))
    return pl.pallas_call(
        _l0_qkv_kernel,
        grid=(NB,),
        in_specs=_halo_specs(M_INNER) + [
            full((CONV_W, M_INNER)), full((1, M_INNER)),
            full((M_HEADS, M_DH, M_DH)), full((M_HEADS, M_DH, M_DH)), full((M_HEADS, M_DH, M_DH)),
            full((3 * M_INNER, LANES)), full((1, LANES)),
        ],
        out_specs=[tok(M_INNER), tok(M_INNER), tok(M_INNER), tok(M_INNER), tok(LANES), tok(LANES)],
        out_shape=[
            jax.ShapeDtypeStruct((N_TOK, M_INNER), F32),
            jax.ShapeDtypeStruct((N_TOK, M_INNER), BF16),
            jax.ShapeDtypeStruct((N_TOK, M_INNER), BF16),
            jax.ShapeDtypeStruct((N_TOK, M_INNER), BF16),
            jax.ShapeDtypeStruct((N_TOK, LANES), F32),
            jax.ShapeDtypeStruct((N_TOK, LANES), F32),
        ],
        scratch_shapes=[pltpu.VMEM((TB + 2 * HALO, M_INNER), F32)],
        compiler_params=_cparams(("arbitrary",)),
        name="l0_qkv",
    )(xm, xm, xm, conv_w, conv_b, wq, wk, wv, wg, bg)


def _mlstm_chunk(r, q, k, v, gh, ght, c_ref, n_ref, m_ref):
    n = q.shape[0]
    lower, upper = _tri_masks(n)
    mask = lower if r == 0 else upper
    li_col, li_row = gh[:, r:r + 1], ght[r:r + 1, :]
    b_col, b_row = gh[:, 2 + r:3 + r], ght[2 + r:3 + r, :]
    m_prev = m_ref[r]
    c_prev = c_ref[r]
    n_prev = n_ref[r]
    d = jnp.where(mask, b_col - b_row + li_row, -jnp.inf)
    g = b_col + m_prev
    m_t = jnp.maximum(g, jnp.max(d, axis=1, keepdims=True))
    qk = lax.dot_general(q, k, (((1,), (1,)), ((), ())), preferred_element_type=F32)
    s = qk * jnp.exp(d - m_t)
    inter = jnp.exp(g - m_t)
    qf = q.astype(F32)
    num = _bdot(s, v) + inter * _bdot(q, c_prev)
    den = jnp.sum(s, axis=1, keepdims=True) + inter * jnp.sum(qf * n_prev, axis=1, keepdims=True)
    h = num / jnp.maximum(jnp.abs(den), jnp.exp(-m_t))
    b_last = b_col[n - 1:n, :] if r == 0 else b_col[0:1, :]
    ld = b_last - b_col + li_col
    m_new = jnp.maximum(b_last + m_prev, jnp.max(ld, axis=0, keepdims=True))
    wk = jnp.exp(ld - m_new)
    decay = jnp.exp(b_last + m_prev - m_new)
    kw = k.astype(F32) * wk
    upd = lax.dot_general(kw.astype(BF16), v, (((0,), (0,)), ((), ())), preferred_element_type=F32)
    c_ref[r] = decay * c_prev + upd
    n_ref[r] = decay * n_prev + jnp.sum(kw, axis=0, keepdims=True)
    m_ref[r] = m_new
    return h


def _mlstm_kernel(qf_ref, kf_ref, vf_ref, ghf_ref, ghtf_ref, qb_ref, kb_ref, vb_ref, ghb_ref, ghtb_ref,
                  c0_ref, n0_ref, m0_ref,
                  hf_ref, hb_ref, co_ref, no_ref, mo_ref,
                  c_ref, n_ref, m_ref):
    s = pl.program_id(1)

    @pl.when(_seq_start(s))
    def _():
        is_ctx = s < NB_CTX
        c_ref[...] = jnp.where(is_ctx, 0.0, c0_ref[...])
        n_ref[...] = jnp.where(is_ctx, 0.0, n0_ref[...])
        m_ref[...] = jnp.where(is_ctx, 0.0, m0_ref[...])

    for c in range(TB // MCH):
        rows = slice(c * MCH, (c + 1) * MCH)
        hf_ref[rows, :] = _mlstm_chunk(0, qf_ref[rows, :], kf_ref[rows, :], vf_ref[rows, :],
                                       ghf_ref[rows, :], ghtf_ref[:, rows], c_ref, n_ref, m_ref)
    for c in reversed(range(TB // MCH)):
        rows = slice(c * MCH, (c + 1) * MCH)
        hb_ref[rows, :] = _mlstm_chunk(1, qb_ref[rows, :], kb_ref[rows, :], vb_ref[rows, :],
                                       ghb_ref[rows, :], ghtb_ref[:, rows], c_ref, n_ref, m_ref)

    @pl.when(s < NB_CTX)
    def _():
        co_ref[...] = c_ref[...]
        no_ref[...] = n_ref[...]
        mo_ref[...] = m_ref[...]


def _mlstm_scan(q, k, v, gh, ght, c0, n0, m0):
    fwd = lambda h, s: (s, h)
    bwd = lambda h, s: (_bwd_block(s), h)
    tokf = pl.BlockSpec((TB, M_DH), fwd)
    tokb = pl.BlockSpec((TB, M_DH), bwd)
    ghf = pl.BlockSpec((None, TB, 4), lambda h, s: (h, s, 0))
    ghb = pl.BlockSpec((None, TB, 4), lambda h, s: (h, _bwd_block(s), 0))
    ghtf = pl.BlockSpec((None, 4, TB), lambda h, s: (h, 0, s))
    ghtb = pl.BlockSpec((None, 4, TB), lambda h, s: (h, 0, _bwd_block(s)))
    lat = _lat_seq_of_block
    ctx = lambda s: jnp.minimum(s, NB_CTX - 1)
    return pl.pallas_call(
        _mlstm_kernel,
        grid=(M_HEADS, NB),
        in_specs=[
            tokf, tokf, tokf, ghf, ghtf, tokb, tokb, tokb, ghb, ghtb,
            pl.BlockSpec((None, None, N_DIR, None, M_DH, M_DH), lambda h, s: (lat(s), 0, 0, h, 0, 0)),
            pl.BlockSpec((None, N_DIR, None, 1, M_DH), lambda h, s: (lat(s), 0, h, 0, 0)),
            pl.BlockSpec((None, N_DIR, None, 1, 1), lambda h, s: (lat(s), 0, h, 0, 0)),
        ],
        out_specs=[
            tokf, tokb,
            pl.BlockSpec((None, None, N_DIR, None, M_DH, M_DH), lambda h, s: (ctx(s), 0, 0, h, 0, 0)),
            pl.BlockSpec((None, N_DIR, None, 1, M_DH), lambda h, s: (ctx(s), 0, h, 0, 0)),
            pl.BlockSpec((None, N_DIR, None, 1, 1), lambda h, s: (ctx(s), 0, h, 0, 0)),
        ],
        out_shape=[
            jax.ShapeDtypeStruct((N_TOK, M_INNER), F32),
            jax.ShapeDtypeStruct((N_TOK, M_INNER), F32),
            jax.ShapeDtypeStruct((BATCH, 1, N_DIR, M_HEADS, M_DH, M_DH), F32),
            jax.ShapeDtypeStruct((BATCH, N_DIR, M_HEADS, 1, M_DH), F32),
            jax.ShapeDtypeStruct((BATCH, N_DIR, M_HEADS, 1, 1), F32),
        ],
        scratch_shapes=[
            pltpu.VMEM((N_DIR, M_DH, M_DH), F32),
            pltpu.VMEM((N_DIR, 1, M_DH), F32),
            pltpu.VMEM((N_DIR, 1, 1), F32),
        ],
        compiler_params=_cparams(("arbitrary", "arbitrary")),
        name="mlstm_scan",
    )(q, k, v, gh, ght, q, k, v, gh, ght, c0, n0, m0)


def _post_ln(x, branch, gate, ln_g, ln_b):
    return _layer_norm(ALPHA * x + gate * branch, ln_g, ln_b)


def _l0_out_kernel(hf_ref, hb_ref, xc_ref, z_ref, nw_ref, sk_ref, w_ref, xa_ref, mod_ref, lg_ref, lb_ref,
                   x1_ref, h2_ref, o_scr):
    for h in range(M_HEADS):
        c0, c1 = h * M_DH, (h + 1) * M_DH
        hs = hf_ref[:, c0:c1] + hb_ref[:, c0:c1]
        mu = jnp.mean(hs, axis=-1, keepdims=True)
        d = hs - mu
        var = jnp.mean(d * d, axis=-1, keepdims=True)
        hn = d * lax.rsqrt(var + LN_EPS) * nw_ref[:, c0:c1]
        o = (hn + sk_ref[:, c0:c1] * xc_ref[:, c0:c1]) * _silu(z_ref[:, c0:c1])
        o_scr[:, c0:c1] = o.astype(BF16)
    out = jnp.dot(o_scr[...], w_ref[...], preferred_element_type=F32)
    x1 = _post_ln(xa_ref[...], out, mod_ref[2:3, :], lg_ref[...], lb_ref[...])
    x1_ref[...] = x1
    h2_ref[...] = (x1 * (1.0 + mod_ref[4:5, :]) + mod_ref[3:4, :]).astype(BF16)


def _l0_out(hf, hb, xc, z, norm_w, skip, w_down, xa, mod, ln_g, ln_b):
    full = lambda shape: pl.BlockSpec(shape, lambda i: (0,) * len(shape))
    tok = lambda w: pl.BlockSpec((TB, w), lambda i: (i, 0))
    return pl.pallas_call(
        _l0_out_kernel,
        grid=(NB,),
        in_specs=[tok(M_INNER), tok(M_INNER), tok(M_INNER), tok(M_INNER),
                  full((1, M_INNER)), full((1, M_INNER)), full((M_INNER, D_MODEL)),
                  tok(D_MODEL), _mod_spec(0, TB), full((1, D_MODEL)), full((1, D_MODEL))],
        out_specs=[tok(D_MODEL), tok(D_MODEL)],
        out_shape=[jax.ShapeDtypeStruct((N_TOK, D_MODEL), F32),
                   jax.ShapeDtypeStruct((N_TOK, D_MODEL), BF16)],
        scratch_shapes=[pltpu.VMEM((TB, M_INNER), BF16)],
        compiler_params=_cparams(("arbitrary",)),
        name="l0_out",
    )(hf, hb, xc, z, norm_w, skip, w_down, xa, mod, ln_g, ln_b)


def _ffn_kernel(h_ref, w1_ref, w3_ref, w2_ref, x_ref, mod_ref, lg_ref, lb_ref, modn_ref,
                x2_ref, h3_ref, acc_ref):
    f = pl.program_id(1)

    @pl.when(f == 0)
    def _():
        acc_ref[...] = jnp.zeros_like(acc_ref)

    h = h_ref[...]
    a = jnp.dot(h, w1_ref[...], preferred_element_type=F32)
    b = jnp.dot(h, w3_ref[...], preferred_element_type=F32)
    acc_ref[...] += _bdot(_silu(a) * b, w2_ref[...])

    @pl.when(f == pl.num_programs(1) - 1)
    def _():
        x2 = _post_ln(x_ref[...], acc_ref[...], mod_ref[5:6, :], lg_ref[...], lb_ref[...])
        x2_ref[...] = x2
        h3_ref[...] = (x2 * (1.0 + modn_ref[1:2, :]) + modn_ref[0:1, :]).astype(BF16)


def _ffn(h2, w1, w3, w2, x1, mod, ln_g, ln_b):
    tok = lambda w: pl.BlockSpec((TM_FF, w), lambda i, f: (i, 0))
    vec = pl.BlockSpec((1, D_MODEL), lambda i, f: (0, 0))
    return pl.pallas_call(
        _ffn_kernel,
        grid=(N_TOK // TM_FF, D_FF // TF_FF),
        in_specs=[tok(D_MODEL),
                  pl.BlockSpec((D_MODEL, TF_FF), lambda i, f: (0, f)),
                  pl.BlockSpec((D_MODEL, TF_FF), lambda i, f: (0, f)),
                  pl.BlockSpec((TF_FF, D_MODEL), lambda i, f: (f, 0)),
                  tok(D_MODEL), _mod_spec(0, TM_FF), vec, vec, _mod_spec(1, TM_FF)],
        out_specs=[tok(D_MODEL), tok(D_MODEL)],
        out_shape=[jax.ShapeDtypeStruct((N_TOK, D_MODEL), F32),
                   jax.ShapeDtypeStruct((N_TOK, D_MODEL), BF16)],
        scratch_shapes=[pltpu.VMEM((TM_FF, D_MODEL), F32)],
        compiler_params=_cparams(("arbitrary", "arbitrary")),
        name="ffn",
    )(h2, w1, w3, w2, x1, mod, ln_g, ln_b, mod)


def _l1_in_kernel(h_ref, wz_ref, wx_ref, wd_ref, z_ref, xbc_ref, dt_ref):
    h = h_ref[...]
    z_ref[...] = jnp.dot(h, wz_ref[...], preferred_element_type=F32)
    xbc_ref[...] = jnp.dot(h, wx_ref[...], preferred_element_type=F32)
    dt_ref[...] = jnp.dot(h, wd_ref[...], preferred_element_type=F32)


def _l1_in(h3, wz, wx, wd):
    full = lambda shape: pl.BlockSpec(shape, lambda i: (0,) * len(shape))
    tok = lambda w: pl.BlockSpec((TB, w), lambda i: (i, 0))
    return pl.pallas_call(
        _l1_in_kernel,
        grid=(NB,),
        in_specs=[tok(D_MODEL), full((D_MODEL, S_INNER)), full((D_MODEL, S_CONV_CH)), full((D_MODEL, LANES))],
        out_specs=[tok(S_INNER), tok(S_CONV_CH), tok(LANES)],
        out_shape=[jax.ShapeDtypeStruct((N_TOK, S_INNER), F32),
                   jax.ShapeDtypeStruct((N_TOK, S_CONV_CH), F32),
                   jax.ShapeDtypeStruct((N_TOK, LANES), F32)],
        compiler_params=_cparams(("arbitrary",)),
        name="l1_in",
    )(h3, wz, wx, wd)


def _l1_conv_kernel(prev_ref, xbc_ref, next_ref, cw_ref, cb_ref, dtr_ref, dtb_ref, a_ref,
                    xs_ref, bmt_ref, cm_ref, dt_ref, cum_ref, ext_ref):
    _fill_ext(ext_ref, prev_ref, xbc_ref, next_ref)
    cw = 512
    for j in range(S_INNER // cw):
        c0, c1 = j * cw, (j + 1) * cw
        xs_ref[:, c0:c1] = _silu(_conv_cols(ext_ref, cw_ref, cb_ref, c0, c1))
    for j in range(S_GN // cw):
        c0, c1 = S_INNER + j * cw, S_INNER + (j + 1) * cw
        bm = _silu(_conv_cols(ext_ref, cw_ref, cb_ref, c0, c1))
        bmt_ref[j * cw:(j + 1) * cw, :] = bm.T.astype(BF16)
        c0, c1 = c0 + S_GN, c1 + S_GN
        cm_ref[:, j * cw:(j + 1) * cw] = _silu(_conv_cols(ext_ref, cw_ref, cb_ref, c0, c1)).astype(BF16)
    x = dtr_ref[...] + dtb_ref[...]
    dt = jnp.maximum(x, 0.0) + jnp.log1p(jnp.exp(-jnp.abs(x)))
    dt_ref[...] = dt
    da = dt * a_ref[...]
    lower, upper = _tri_masks()
    lane_c = lax.broadcasted_iota(jnp.int32, (CHUNK, LANES), 1)
    for c in range(CPB):
        dc = da[c * CHUNK:(c + 1) * CHUNK, :]
        pref = _tri_dot(lower.astype(BF16), dc)
        suff = _tri_dot(upper.astype(BF16), dc)
        cum_ref[c * CHUNK:(c + 1) * CHUNK, :] = jnp.where(lane_c < S_HEADS, pref, suff)


def _l1_conv(xbc, conv_w, conv_b, dt_raw, dt_bias, a_neg):
    full = lambda shape: pl.BlockSpec(shape, lambda i: (0,) * len(shape))
    tok = lambda w: pl.BlockSpec((TB, w), lambda i: (i, 0))
    return pl.pallas_call(
        _l1_conv_kernel,
        grid=(NB,),
        in_specs=_halo_specs(S_CONV_CH) + [
            full((CONV_W, S_CONV_CH)), full((1, S_CONV_CH)), tok(LANES), full((1, LANES)), full((1, LANES))],
        out_specs=[tok(S_INNER), pl.BlockSpec((S_GN, TB), lambda i: (0, i)), tok(S_GN), tok(LANES), tok(LANES)],
        out_shape=[jax.ShapeDtypeStruct((N_TOK, S_INNER), F32),
                   jax.ShapeDtypeStruct((S_GN, N_TOK), BF16),
                   jax.ShapeDtypeStruct((N_TOK, S_GN), BF16),
                   jax.ShapeDtypeStruct((N_TOK, LANES), F32),
                   jax.ShapeDtypeStruct((N_TOK, LANES), F32)],
        scratch_shapes=[pltpu.VMEM((TB + 2 * HALO, S_CONV_CH), F32)],
        compiler_params=_cparams(("arbitrary",)),
        name="l1_conv",
    )(xbc, xbc, xbc, conv_w, conv_b, dt_raw, dt_bias, a_neg)


GW = S_HPG * S_HEADDIM


def _ssd_chunk(r, x, bmt, cm, dg, dgt, sel, dsk, st_ref, y_ref, rows):
    lower, upper = _tri_masks()
    mask = lower if r == 0 else upper
    st = st_ref[r]
    hi = dg.astype(BF16)
    r1 = dg - hi.astype(F32)
    mid = r1.astype(BF16)
    lo = (r1 - mid.astype(F32)).astype(BF16)
    spread = lambda a: jnp.dot(a, sel, preferred_element_type=F32)
    bro = spread(hi) + spread(mid) + spread(lo)
    cum_b = bro[:, S_HPG * CHUNK:S_HPG * CHUNK + GW]
    dt_b = bro[:, S_HPG * CHUNK + GW:S_HPG * CHUNK + 2 * GW]
    cb = jnp.dot(cm, bmt, preferred_element_type=F32)
    lane = lax.broadcasted_iota(jnp.int32, (CHUNK, GW), 1)
    ws, xd = [], []
    for j in range(S_HPG):
        c = S_HPG * r + j
        seg = bro[:, j * CHUNK:(j + 1) * CHUNK] - dgt[2 * S_HPG + c:2 * S_HPG + c + 1, :]
        w = jnp.exp(jnp.where(mask, seg, -jnp.inf)) * cb * dgt[c:c + 1, :]
        ws.append(w.astype(BF16))
        head = jnp.logical_and(lane >= j * S_HEADDIM, lane < (j + 1) * S_HEADDIM)
        xd.append(jnp.where(head, x, 0.0).astype(BF16))
    y = jnp.dot(jnp.concatenate(ws, axis=1), jnp.concatenate(xd, axis=0), preferred_element_type=F32)
    y = y + jnp.exp(cum_b) * _bdot(cm, st)
    if r == 0:
        y = y + dsk * x
    y_ref[rows, :] = y
    last_b = cum_b[CHUNK - 1:CHUNK, :] if r == 0 else cum_b[0:1, :]
    xw = x * (jnp.exp(last_b - cum_b) * dt_b)
    st_ref[r] = jnp.exp(last_b) * st + _bdot(bmt, xw)


def _ssd_kernel(xf_ref, bf_ref, cf_ref, dgf_ref, dtf_ref, xb_ref, bb_ref, cb_ref, dgb_ref, dtb_ref,
                sel_ref, dsk_ref, s0_ref, yf_ref, yb_ref, so_ref, st_ref):
    s = pl.program_id(1)

    @pl.when(_seq_start(s))
    def _():
        is_ctx = s < NB_CTX
        for r in range(N_DIR):
            s0 = jnp.concatenate([s0_ref[r, j] for j in range(S_HPG)], axis=0)
            st_ref[r] = jnp.where(is_ctx, 0.0, s0.T)

    dsk = dsk_ref[...]
    for c in range(CPB):
        rows = slice(c * CHUNK, (c + 1) * CHUNK)
        _ssd_chunk(0, xf_ref[rows, :], bf_ref[:, rows], cf_ref[rows, :], dgf_ref[rows, :], dtf_ref[:, rows],
                   sel_ref[0], dsk, st_ref, yf_ref, rows)
    for c in reversed(range(CPB)):
        rows = slice(c * CHUNK, (c + 1) * CHUNK)
        _ssd_chunk(1, xb_ref[rows, :], bb_ref[:, rows], cb_ref[rows, :], dgb_ref[rows, :], dtb_ref[:, rows],
                   sel_ref[1], dsk, st_ref, yb_ref, rows)

    @pl.when(s < NB_CTX)
    def _():
        for r in range(N_DIR):
            stt = st_ref[r].T
            for j in range(S_HPG):
                so_ref[r, j] = stt[j * S_HEADDIM:(j + 1) * S_HEADDIM, :]


def _ssd_selector():
    ng = 2 * N_DIR * S_HPG
    sel = np.zeros((N_DIR, ng, S_HPG * CHUNK + 2 * GW), np.float32)
    for r in range(N_DIR):
        for j in range(S_HPG):
            c = S_HPG * r + j
            sel[r, N_DIR * S_HPG + c, j * CHUNK:(j + 1) * CHUNK] = 1.0
            sel[r, N_DIR * S_HPG + c, S_HPG * CHUNK + j * S_HEADDIM:S_HPG * CHUNK + (j + 1) * S_HEADDIM] = 1.0
            sel[r, c, S_HPG * CHUNK + GW + j * S_HEADDIM:S_HPG * CHUNK + GW + (j + 1) * S_HEADDIM] = 1.0
    return jnp.asarray(sel, dtype=BF16)


def _ssd_scan(xs, bmt, cm, dg, dgt, dsk, s0):
    ng = 2 * N_DIR * S_HPG
    def side(blk):
        return [
            pl.BlockSpec((TB, GW), lambda g, s: (blk(s), g)),
            pl.BlockSpec((S_STATE, TB), lambda g, s: (g, blk(s))),
            pl.BlockSpec((TB, S_STATE), lambda g, s: (blk(s), g)),
            pl.BlockSpec((None, TB, ng), lambda g, s: (g, blk(s), 0)),
            pl.BlockSpec((None, ng, TB), lambda g, s: (g, 0, blk(s))),
        ]
    ident = lambda s: s
    lat = _lat_seq_of_block
    ctx = lambda s: jnp.minimum(s, NB_CTX - 1)
    st_block = (None, None, N_DIR, S_HPG, S_HEADDIM, S_STATE)
    sel = _ssd_selector()
    return pl.pallas_call(
        _ssd_kernel,
        grid=(S_GROUPS, NB),
        in_specs=side(ident) + side(_bwd_block) + [
            pl.BlockSpec(sel.shape, lambda g, s: (0, 0, 0)),
            pl.BlockSpec((None, 1, GW), lambda g, s: (g, 0, 0)),
            pl.BlockSpec(st_block, lambda g, s: (lat(s), 0, 0, g, 0, 0)),
        ],
        out_specs=[
            pl.BlockSpec((TB, GW), lambda g, s: (s, g)),
            pl.BlockSpec((TB, GW), lambda g, s: (_bwd_block(s), g)),
            pl.BlockSpec(st_block, lambda g, s: (ctx(s), 0, 0, g, 0, 0)),
        ],
        out_shape=[
            jax.ShapeDtypeStruct((N_TOK, S_INNER), F32),
            jax.ShapeDtypeStruct((N_TOK, S_INNER), F32),
            jax.ShapeDtypeStruct((BATCH, 1, N_DIR, S_HEADS, S_HEADDIM, S_STATE), F32),
        ],
        scratch_shapes=[pltpu.VMEM((N_DIR, S_STATE, GW), F32)],
        compiler_params=_cparams(("arbitrary", "arbitrary")),
        name="ssd_scan",
    )(xs, bmt, cm, dg, dgt, xs, bmt, cm, dg, dgt, sel, dsk, s0)


def _l1_out_kernel(yf_ref, yb_ref, z_ref, nw_ref, w_ref, x_ref, mod_ref, lg_ref, lb_ref, wr_ref,
                   x3_ref, h4_ref, gates_ref):
    y = (yf_ref[...] + yb_ref[...]) * _silu(z_ref[...])
    y = y * lax.rsqrt(jnp.mean(y * y, axis=-1, keepdims=True) + RMS_EPS) * nw_ref[...]
    out = _bdot(y, w_ref[...])
    x3 = _post_ln(x_ref[...], out, mod_ref[2:3, :], lg_ref[...], lb_ref[...])
    x3_ref[...] = x3
    h4 = x3 * (1.0 + mod_ref[4:5, :]) + mod_ref[3:4, :]
    for cc in range(ROW_TILE):
        h4_ref[pl.ds(cc, TB, stride=ROW_TILE), :] = h4[:, cc * LANES:(cc + 1) * LANES]
    lane = lax.broadcasted_iota(jnp.int32, (TB, LANES), 1)
    logits = jnp.full((TB, LANES), -jnp.inf, F32)
    for e in range(N_EXPERTS):
        le = jnp.sum(h4 * wr_ref[e:e + 1, :], axis=-1, keepdims=True)
        logits = jnp.where(lane == e, le, logits)
    v1 = jnp.max(logits, axis=-1, keepdims=True)
    i1 = jnp.min(jnp.where(logits == v1, lane, LANES), axis=-1, keepdims=True)
    rest = jnp.where(lane == i1, -jnp.inf, logits)
    v2 = jnp.max(rest, axis=-1, keepdims=True)
    i2 = jnp.min(jnp.where(rest == v2, lane, LANES), axis=-1, keepdims=True)
    e2 = jnp.exp(v2 - v1)
    w1 = 1.0 / (1.0 + e2)
    w2 = e2 / (1.0 + e2)
    route = jnp.where(lane == 0, w1, jnp.where(lane == 1, w2, 0.0))
    route = jnp.where(lane == 2, i1.astype(F32), jnp.where(lane == 3, i2.astype(F32), route))
    gates_ref[...] = route


def _l1_out(yf, yb, z, norm_w, w_out, x2, mod, ln_g, ln_b, wr_t):
    full = lambda shape: pl.BlockSpec(shape, lambda i: (0,) * len(shape))
    tok = lambda w: pl.BlockSpec((TB, w), lambda i: (i, 0))
    return pl.pallas_call(
        _l1_out_kernel,
        grid=(NB,),
        in_specs=[tok(S_INNER), tok(S_INNER), tok(S_INNER), full((1, S_INNER)), full((S_INNER, D_MODEL)),
                  tok(D_MODEL), _mod_spec(1, TB), full((1, D_MODEL)), full((1, D_MODEL)),
                  full((N_EXPERTS, D_MODEL))],
        out_specs=[tok(D_MODEL), pl.BlockSpec((TB * ROW_TILE, LANES), lambda i: (i, 0)), tok(LANES)],
        out_shape=[jax.ShapeDtypeStruct((N_TOK, D_MODEL), F32),
                   jax.ShapeDtypeStruct((N_TOK * ROW_TILE, LANES), F32),
                   jax.ShapeDtypeStruct((N_TOK, LANES), F32)],
        compiler_params=_cparams(("arbitrary",)),
        name="l1_out",
    )(yf, yb, z, norm_w, w_out, x2, mod, ln_g, ln_b, wr_t)


TM_E = 512
N_ASSIGN = TOP_K * N_TOK
P_ROWS = N_ASSIGN + N_EXPERTS * TM_E
N_PT = P_ROWS // TM_E
ROW_TILE = D_MODEL // LANES
NF_MOE = E_FF // TF_MOE


def _row(i):
    return pl.ds(pl.multiple_of(i * ROW_TILE, ROW_TILE), ROW_TILE)


def _expert_kernel(te_ref, nv_ref, src_ref, srcn_ref, asg_ref, h_ref, w1_ref, w3_ref, w2_ref, yc_ref,
                   xbuf, xb_ref, acc_ref, ybuf, gsem, ssem):
    j, f = pl.program_id(0), pl.program_id(1)
    n_valid = nv_ref[0]
    valid = j < n_valid
    last_f = f == NF_MOE - 1

    def start_gather(idx_ref):
        def body(r, carry):
            pltpu.make_async_copy(h_ref.at[_row(idx_ref[0, r])], xbuf.at[_row(r)], gsem).start()
            return carry
        lax.fori_loop(0, TM_E, body, 0, unroll=8)

    def wait_gather():
        pltpu.make_async_copy(h_ref.at[pl.ds(0, TM_E * ROW_TILE)], xbuf, gsem).wait()

    def wait_scatter():
        pltpu.make_async_copy(ybuf, yc_ref.at[pl.ds(0, TM_E * ROW_TILE)], ssem).wait()

    @pl.when(jnp.logical_and(j == 0, f == 0))
    def _():
        start_gather(src_ref)

    @pl.when(jnp.logical_and(valid, f == 0))
    def _():
        wait_gather()
        for cc in range(ROW_TILE):
            xb_ref[:, cc * LANES:(cc + 1) * LANES] = xbuf[pl.ds(cc, TM_E, stride=ROW_TILE), :].astype(BF16)
        acc_ref[...] = jnp.zeros_like(acc_ref)

    @pl.when(jnp.logical_and(last_f, j + 1 < n_valid))
    def _():
        start_gather(srcn_ref)

    @pl.when(valid)
    def _():
        x = xb_ref[...]
        a = jnp.dot(x, w1_ref[...], preferred_element_type=F32)
        b = jnp.dot(x, w3_ref[...], preferred_element_type=F32)
        acc_ref[...] += _bdot(_silu(a) * b, w2_ref[...])

    @pl.when(last_f)
    def _():
        @pl.when(j > 0)
        def _():
            wait_scatter()

        for cc in range(ROW_TILE):
            ybuf[pl.ds(cc, TM_E, stride=ROW_TILE), :] = jnp.where(valid, acc_ref[:, cc * LANES:(cc + 1) * LANES], 0.0)

        def body(r, carry):
            pltpu.make_async_copy(ybuf.at[_row(r)], yc_ref.at[_row(asg_ref[0, r])], ssem).start()
            return carry
        lax.fori_loop(0, TM_E, body, 0, unroll=8)

        @pl.when(j == N_PT - 1)
        def _():
            wait_scatter()


def _experts(tile_expert, n_valid, src, asg, h4, w1, w3, w2):
    fidx = lambda j, f, nv: jnp.where(j < nv[0], f, NF_MOE - 1)
    smem_tile = lambda off: pl.BlockSpec((None, 1, TM_E), lambda j, f, te, nv: (jnp.minimum(j + off, N_PT - 1), 0, 0),
                                         memory_space=pltpu.SMEM)
    src3 = src.reshape(N_PT, 1, TM_E)
    return pl.pallas_call(
        _expert_kernel,
        grid_spec=pltpu.PrefetchScalarGridSpec(
            num_scalar_prefetch=2,
            grid=(N_PT, NF_MOE),
            in_specs=[smem_tile(0), smem_tile(1), smem_tile(0),
                      pl.BlockSpec(memory_space=pl.ANY),
                      pl.BlockSpec((None, D_MODEL, TF_MOE), lambda j, f, te, nv: (te[j], 0, fidx(j, f, nv))),
                      pl.BlockSpec((None, D_MODEL, TF_MOE), lambda j, f, te, nv: (te[j], 0, fidx(j, f, nv))),
                      pl.BlockSpec((None, TF_MOE, D_MODEL), lambda j, f, te, nv: (te[j], fidx(j, f, nv), 0))],
            out_specs=pl.BlockSpec(memory_space=pl.ANY),
            scratch_shapes=[pltpu.VMEM((TM_E * ROW_TILE, LANES), F32),
                            pltpu.VMEM((TM_E, D_MODEL), BF16),
                            pltpu.VMEM((TM_E, D_MODEL), F32),
                            pltpu.VMEM((TM_E * ROW_TILE, LANES), F32),
                            pltpu.SemaphoreType.DMA(()),
                            pltpu.SemaphoreType.DMA(())],
        ),
        out_shape=jax.ShapeDtypeStruct((P_ROWS * ROW_TILE, LANES), F32),
        compiler_params=_cparams(("arbitrary", "arbitrary")),
        name="experts",
    )(tile_expert, n_valid, src3, src3, asg.reshape(N_PT, 1, TM_E), h4, w1, w3, w2)


def _moe_out_kernel(yc_ref, r_ref, x_ref, mod_ref, lg_ref, lb_ref, yp_ref, ys_ref, f_ref):
    i = pl.program_id(0)
    w1, w2 = r_ref[:, 0:1], r_ref[:, 1:2]
    for cc in range(ROW_TILE):
        y_top1 = yc_ref[pl.ds(cc, TM_FF, stride=TOP_K * ROW_TILE), :]
        y_top2 = yc_ref[pl.ds(ROW_TILE + cc, TM_FF, stride=TOP_K * ROW_TILE), :]
        f_ref[:, cc * LANES:(cc + 1) * LANES] = w1 * y_top1 + w2 * y_top2
    y = _post_ln(x_ref[...], f_ref[...], mod_ref[5:6, :], lg_ref[...], lb_ref[...])

    @pl.when(i < N_CTX // TM_FF)
    def _():
        yp_ref[...] = y

    @pl.when(i >= N_CTX // TM_FF)
    def _():
        ys_ref[...] = y


def _moe_out(yc, route, x3, mod, ln_g, ln_b):
    nctx = N_CTX // TM_FF
    tok = lambda w: pl.BlockSpec((TM_FF, w), lambda i: (i, 0))
    vec = pl.BlockSpec((1, D_MODEL), lambda i: (0, 0))
    return pl.pallas_call(
        _moe_out_kernel,
        grid=(N_TOK // TM_FF,),
        in_specs=[pl.BlockSpec((TM_FF * TOP_K * ROW_TILE, LANES), lambda i: (i, 0)),
                  tok(LANES), tok(D_MODEL), _mod_spec(1, TM_FF), vec, vec],
        out_specs=[pl.BlockSpec((TM_FF, D_MODEL), lambda i: (jnp.minimum(i, nctx - 1), 0)),
                   pl.BlockSpec((TM_FF, D_MODEL), lambda i: (jnp.maximum(i - nctx, 0), 0))],
        out_shape=[jax.ShapeDtypeStruct((N_CTX, D_MODEL), F32),
                   jax.ShapeDtypeStruct((N_LAT, D_MODEL), F32)],
        scratch_shapes=[pltpu.VMEM((TM_FF, D_MODEL), F32)],
        compiler_params=_cparams(("arbitrary",)),
        name="moe_out",
    )(yc, route, x3, mod, ln_g, ln_b)


def _routing_tables(route):
    flat_e = route[:, 2:2 + TOP_K].astype(jnp.int32).reshape(N_ASSIGN)
    onehot = (flat_e[:, None] == jnp.arange(N_EXPERTS, dtype=jnp.int32)[None, :]).astype(jnp.int32)
    csum = jnp.cumsum(onehot, axis=0)
    rank = jnp.take_along_axis(csum, flat_e[:, None], axis=1)[:, 0] - 1
    ntile = (csum[-1] + TM_E - 1) // TM_E
    tile_end = jnp.cumsum(ntile)
    dest = (tile_end - ntile)[flat_e] * TM_E + rank
    asg = jnp.full((P_ROWS,), -1, jnp.int32).at[dest].set(jnp.arange(N_ASSIGN, dtype=jnp.int32))
    is_pad = asg < 0
    src = jnp.maximum(asg, 0) // TOP_K
    asg = jnp.where(is_pad, N_ASSIGN - 1 + jnp.cumsum(is_pad.astype(jnp.int32)), asg)
    tiles = jnp.arange(N_PT, dtype=jnp.int32)
    n_valid = tile_end[-1]
    tile_expert = jnp.sum((jnp.minimum(tiles, n_valid - 1)[:, None] >= tile_end[None, :]).astype(jnp.int32), axis=1)
    return src, asg, tile_expert, n_valid.reshape(1)


def _grid_pos_embed():
    rows = DEC_SEQ // GRID_W
    quarter = D_MODEL // 4
    freqs = jnp.exp(-math.log(POS_BASE) * jnp.arange(quarter, dtype=F32) / quarter)
    er = jnp.arange(rows, dtype=F32)[:, None] * freqs
    ec = jnp.arange(GRID_W, dtype=F32)[:, None] * freqs
    row_emb = jnp.concatenate([jnp.sin(er), jnp.cos(er)], axis=-1)
    col_emb = jnp.concatenate([jnp.sin(ec), jnp.cos(ec)], axis=-1)
    pos = jnp.concatenate([jnp.broadcast_to(row_emb[:, None, :], (rows, GRID_W, D_MODEL // 2)),
                           jnp.broadcast_to(col_emb[None, :, :], (rows, GRID_W, D_MODEL // 2))], axis=-1)
    return pos.reshape(DEC_SEQ, D_MODEL)


def _pad_lanes(a, width=LANES):
    return jnp.pad(a, [(0, 0)] * (a.ndim - 1) + [(0, width - a.shape[-1])])


def kernel(x_prompt, x_sample, c, state_mlstm_C, state_mlstm_n, state_mlstm_m, state_ssm, c_ctx,
           ada_w, ada_b, ln_g, ln_b,
           ml_w_up, ml_conv_w, ml_conv_b, ml_wq, ml_wk, ml_wv, ml_w_ig, ml_b_ig, ml_w_fg, ml_b_fg,
           ml_norm_w, ml_skip, ml_w_down,
           ss_w_in, ss_conv_w, ss_conv_b, ss_dt_bias, ss_a_log, ss_d, ss_norm_w, ss_w_out,
           ff_w1, ff_w3, ff_w2,
           moe_w_router, moe_w1, moe_w3, moe_w2):
    row = lambda a: a.reshape(1, -1)
    cond = jnp.concatenate([c_ctx[None, :], c, jnp.zeros((COND_ROWS - N_COND, D_MODEL), F32)], axis=0)
    mod = _ada_mod(cond, ada_w.astype(BF16), ada_b)

    xa, xm, z0 = _l0_in(x_prompt.reshape(N_CTX, D_MODEL), x_sample.reshape(N_LAT, D_MODEL), _grid_pos_embed(),
                        mod, ml_w_up[0].astype(BF16))
    ng = N_DIR * M_HEADS
    wg = jnp.concatenate([ml_w_ig[0].transpose(1, 0, 2).reshape(3 * M_INNER, ng),
                          ml_w_fg[0].transpose(1, 0, 2).reshape(3 * M_INNER, ng)], axis=1)
    bg = jnp.concatenate([ml_b_ig[0].reshape(ng), ml_b_fg[0].reshape(ng)])
    xc, q, k, v, g, bc = _l0_qkv(xm, ml_conv_w[0], row(ml_conv_b[0]),
                                 ml_wq[0].astype(BF16), ml_wk[0].astype(BF16), ml_wv[0].astype(BF16),
                                 _pad_lanes(wg).astype(BF16), row(_pad_lanes(bg)))
    gh = jnp.concatenate([g[:, :ng].reshape(N_TOK, N_DIR, M_HEADS),
                          bc[:, ng:2 * ng].reshape(N_TOK, N_DIR, M_HEADS)], axis=1)
    ght = gh.transpose(2, 1, 0)
    gh = gh.transpose(2, 0, 1)
    hf, hb, new_c, new_n, new_m = _mlstm_scan(
        q, k, v, gh, ght, state_mlstm_C,
        state_mlstm_n.reshape(DEC_BATCH, N_DIR, M_HEADS, 1, M_DH),
        state_mlstm_m.reshape(DEC_BATCH, N_DIR, M_HEADS, 1, 1))
    x1, h2 = _l0_out(hf, hb, xc, z0, row(ml_norm_w[0]), row(ml_skip[0]), ml_w_down[0].astype(BF16),
                     xa, mod, row(ln_g[0, 0]), row(ln_b[0, 0]))
    x2, h3 = _ffn(h2, ff_w1[0].astype(BF16), ff_w3[0].astype(BF16), ff_w2[0].astype(BF16),
                  x1, mod, row(ln_g[0, 1]), row(ln_b[0, 1]))

    w_in = ss_w_in[0]
    z1, xbc, dt_raw = _l1_in(h3, w_in[:, :S_INNER].astype(BF16),
                             w_in[:, S_INNER:S_INNER + S_CONV_CH].astype(BF16),
                             _pad_lanes(w_in[:, S_INNER + S_CONV_CH:]).astype(BF16))
    a_neg = -jnp.exp(ss_a_log[0].reshape(N_DIR * S_HEADS))
    xs, bmt, cm, dt, cum = _l1_conv(xbc, ss_conv_w[0], row(ss_conv_b[0]), dt_raw,
                                    row(_pad_lanes(ss_dt_bias[0].reshape(N_DIR * S_HEADS))), row(_pad_lanes(a_neg)))
    nh = N_DIR * S_HEADS
    to_group = lambda a: a[:, :nh].reshape(N_TOK, N_DIR, S_GROUPS, S_HPG).transpose(2, 0, 1, 3).reshape(
        S_GROUPS, N_TOK, N_DIR * S_HPG)
    dg = jnp.concatenate([to_group(dt), to_group(cum)], axis=-1)
    dgt = dg.transpose(0, 2, 1)
    dsk = jnp.repeat(ss_d[0], S_HEADDIM).reshape(S_GROUPS, 1, GW)
    yf, yb, new_s = _ssd_scan(xs, bmt, cm, dg, dgt, dsk, state_ssm)
    x3, h4, route = _l1_out(yf, yb, z1, row(ss_norm_w[0]), ss_w_out[0].astype(BF16), x2, mod,
                            row(ln_g[1, 0]), row(ln_b[1, 0]), moe_w_router[0].T)
    src, asg, tile_expert, n_valid = _routing_tables(route)
    yc = _experts(tile_expert, n_valid, src, asg, h4,
                  moe_w1[0].astype(BF16), moe_w3[0].astype(BF16), moe_w2[0].astype(BF16))
    yp, ys = _moe_out(yc, route, x3, mod, row(ln_g[1, 1]), row(ln_b[1, 1]))

    return (yp.reshape(BATCH, SEQ, D_MODEL), ys.reshape(DEC_BATCH, DEC_SEQ, D_MODEL),
            new_c,
            new_n.reshape(BATCH, 1, N_DIR, M_HEADS, M_DH),
            new_m.reshape(BATCH, 1, N_DIR, M_HEADS),
            new_s)
```

```python
import functools
import math

import jax
import jax.numpy as jnp
import numpy as np
from jax import lax
from jax.experimental import pallas as pl
from jax.experimental.pallas import tpu as pltpu

F32 = jnp.float32
BF16 = jnp.bfloat16

D_MODEL = 1024
BATCH = 16
SEQ = 256
DEC_BATCH = 4
DEC_SEQ = 4096
GRID_W = 64
CHUNK = 128
CONV_W = 5
N_DIR = 2
M_INNER = 2 * D_MODEL
M_HEADS = 4
M_DH = M_INNER // M_HEADS
S_INNER = 2 * D_MODEL
S_HEADDIM = 64
S_HEADS = S_INNER // S_HEADDIM
S_GROUPS = 8
S_HPG = S_HEADS // S_GROUPS
S_STATE = 128
S_GN = S_GROUPS * S_STATE
S_CONV_CH = S_INNER + 2 * S_GN
D_FF = 2816
N_EXPERTS = 8
TOP_K = 2
E_FF = 3584
DEPTH = 2
ALPHA = (2 * DEPTH) ** 0.25
LN_EPS = 1e-5
RMS_EPS = 1e-5
POS_BASE = 10000.0

N_CTX = BATCH * SEQ
N_LAT = DEC_BATCH * DEC_SEQ
N_TOK = N_CTX + N_LAT
N_COND = 1 + DEC_BATCH
COND_ROWS = 8

TB = 256
NB = N_TOK // TB
NB_CTX = N_CTX // TB
NB_SEQ = DEC_SEQ // TB
HALO = 8
CPB = TB // CHUNK
MCH = TB

TM_FF = 512
TF_FF = 1408
TF_MOE = 1792
LANES = 128

VMEM_LIMIT = 56 * 1024 * 1024


def _cparams(sem):
    return pltpu.CompilerParams(dimension_semantics=sem, vmem_limit_bytes=VMEM_LIMIT)


def _cond_of_token(t0):
    return jnp.where(t0 < N_CTX, 0, 1 + (t0 - N_CTX) // DEC_SEQ)


def _lat_seq_of_block(s):
    return jnp.maximum(s - NB_CTX, 0) // NB_SEQ


def _bwd_block(s):
    k = s - NB_CTX
    return jnp.where(s < NB_CTX, s, NB_CTX + (k // NB_SEQ) * NB_SEQ + (NB_SEQ - 1 - k % NB_SEQ))


def _seq_start(s):
    return jnp.logical_or(s < NB_CTX, (s - NB_CTX) % NB_SEQ == 0)


def _seq_end(s):
    return jnp.logical_or(s < NB_CTX, (s - NB_CTX) % NB_SEQ == NB_SEQ - 1)


def _silu(x):
    return x * jax.nn.sigmoid(x)


def _bdot(a, b):
    return jnp.dot(a.astype(BF16), b.astype(BF16), preferred_element_type=F32)


def _tri_dot(tri, x):
    hi = x.astype(BF16)
    r1 = x - hi.astype(F32)
    mid = r1.astype(BF16)
    lo = (r1 - mid.astype(F32)).astype(BF16)
    d = lambda a: jnp.dot(tri, a, preferred_element_type=F32)
    return d(hi) + d(mid) + d(lo)


def _tri_masks(n=CHUNK):
    t = lax.broadcasted_iota(jnp.int32, (n, n), 0)
    u = lax.broadcasted_iota(jnp.int32, (n, n), 1)
    return u <= t, u >= t


def _layer_norm(r, g, b):
    mu = jnp.mean(r, axis=-1, keepdims=True)
    d = r - mu
    var = jnp.mean(d * d, axis=-1, keepdims=True)
    return d * lax.rsqrt(var + LN_EPS) * g + b


def _ada_kernel(cond_ref, w_ref, b_ref, o_ref):
    o_ref[...] = _bdot(_silu(cond_ref[...]), w_ref[...]) + b_ref[...]


def _ada_mod(cond, ada_w, ada_b):
    ncol = 6 * D_MODEL // D_MODEL
    out = pl.pallas_call(
        _ada_kernel,
        grid=(DEPTH, ncol),
        in_specs=[
            pl.BlockSpec((COND_ROWS, D_MODEL), lambda l, j: (0, 0)),
            pl.BlockSpec((None, D_MODEL, D_MODEL), lambda l, j: (l, 0, j)),
            pl.BlockSpec((None, 1, D_MODEL), lambda l, j: (l, 0, j)),
        ],
        out_specs=pl.BlockSpec((None, COND_ROWS, D_MODEL), lambda l, j: (l, 0, j)),
        out_shape=jax.ShapeDtypeStruct((DEPTH, COND_ROWS, 6 * D_MODEL), F32),
        compiler_params=_cparams(("arbitrary", "arbitrary")),
        name="ada_mod",
    )(cond, ada_w, ada_b.reshape(DEPTH, 1, 6 * D_MODEL))
    return out.reshape(DEPTH, COND_ROWS, 6, D_MODEL)


def _mod_spec(layer, tm):
    return pl.BlockSpec((None, None, 6, D_MODEL), lambda i, *_: (layer, _cond_of_token(i * tm), 0, 0))


def _l0_in_kernel(xp_ref, xs_ref, pos_ref, mod_ref, w_ref, xa_ref, xm_ref, z_ref):
    i = pl.program_id(0)
    x = jnp.where(i >= NB_CTX, xs_ref[...] + pos_ref[...], xp_ref[...])
    xa_ref[...] = x
    h = x * (1.0 + mod_ref[1:2, :]) + mod_ref[0:1, :]
    u = _bdot(h, w_ref[...])
    xm_ref[...] = u[:, :M_INNER]
    z_ref[...] = u[:, M_INNER:].astype(BF16)


def _l0_in(xp, xs, pos, mod, w_up):
    return pl.pallas_call(
        _l0_in_kernel,
        grid=(NB,),
        in_specs=[
            pl.BlockSpec((TB, D_MODEL), lambda i: (jnp.minimum(i, NB_CTX - 1), 0)),
            pl.BlockSpec((TB, D_MODEL), lambda i: (jnp.maximum(i - NB_CTX, 0), 0)),
            pl.BlockSpec((TB, D_MODEL), lambda i: (jnp.maximum(i - NB_CTX, 0) % NB_SEQ, 0)),
            _mod_spec(0, TB),
            pl.BlockSpec((D_MODEL, 2 * M_INNER), lambda i: (0, 0)),
        ],
        out_specs=[
            pl.BlockSpec((TB, D_MODEL), lambda i: (i, 0)),
            pl.BlockSpec((TB, M_INNER), lambda i: (i, 0)),
            pl.BlockSpec((TB, M_INNER), lambda i: (i, 0)),
        ],
        out_shape=[
            jax.ShapeDtypeStruct((N_TOK, D_MODEL), F32),
            jax.ShapeDtypeStruct((N_TOK, M_INNER), F32),
            jax.ShapeDtypeStruct((N_TOK, M_INNER), BF16),
        ],
        compiler_params=_cparams(("arbitrary",)),
        name="l0_in",
    )(xp, xs, pos, mod, w_up)


def _fill_ext(ext_ref, prev_ref, main_ref, next_ref):
    i = pl.program_id(0)
    ext_ref[0:HALO, :] = jnp.where(_seq_start(i), 0.0, prev_ref[...])
    ext_ref[HALO:HALO + TB, :] = main_ref[...]
    ext_ref[HALO + TB:HALO + TB + HALO, :] = jnp.where(_seq_end(i), 0.0, next_ref[...])


def _conv_cols(ext_ref, w_ref, b_ref, c0, c1):
    acc = b_ref[:, c0:c1]
    for j in range(CONV_W):
        r0 = HALO - CONV_W // 2 + j
        acc = acc + w_ref[j:j + 1, c0:c1] * ext_ref[r0:r0 + TB, c0:c1]
    return acc


def _halo_specs(width):
    nh = TB // HALO
    last = N_TOK // HALO - 1
    return [
        pl.BlockSpec((HALO, width), lambda i: (jnp.maximum(i * nh - 1, 0), 0)),
        pl.BlockSpec((TB, width), lambda i: (i, 0)),
        pl.BlockSpec((HALO, width), lambda i: (jnp.minimum((i + 1) * nh, last), 0)),
    ]


def _l0_qkv_kernel(prev_ref, xm_ref, next_ref, cw_ref, cb_ref, wq_ref, wk_ref, wv_ref, wg_ref, bg_ref,
                   xc_ref, q_ref, k_ref, v_ref, g_ref, bc_ref, ext_ref):
    _fill_ext(ext_ref, prev_ref, xm_ref, next_ref)
    pre = jnp.zeros((TB, LANES), F32) + bg_ref[...]
    for h in range(M_HEADS):
        c0, c1 = h * M_DH, (h + 1) * M_DH
        xc = _silu(_conv_cols(ext_ref, cw_ref, cb_ref, c0, c1))
        xcb = xc.astype(BF16)
        xc_ref[:, c0:c1] = xcb
        q = jnp.dot(xcb, wq_ref[h], preferred_element_type=F32).astype(BF16)
        k = (jnp.dot(xcb, wk_ref[h], preferred_element_type=F32) * (M_DH ** -0.5)).astype(BF16)
        v = jnp.dot(xm_ref[:, c0:c1].astype(BF16), wv_ref[h], preferred_element_type=F32).astype(BF16)
        q_ref[:, c0:c1] = q
        k_ref[:, c0:c1] = k
        v_ref[:, c0:c1] = v
        pre = pre + jnp.dot(q, wg_ref[c0:c1, :], preferred_element_type=F32)
        pre = pre + jnp.dot(k, wg_ref[M_INNER + c0:M_INNER + c1, :], preferred_element_type=F32)
        pre = pre + jnp.dot(v, wg_ref[2 * M_INNER + c0:2 * M_INNER + c1, :], preferred_element_type=F32)
    lane = lax.broadcasted_iota(jnp.int32, (TB, LANES), 1)
    ng = N_DIR * M_HEADS
    logsig = -(jnp.maximum(-pre, 0.0) + jnp.log1p(jnp.exp(-jnp.abs(pre))))
    g = jnp.where(jnp.logical_and(lane >= ng, lane < 2 * ng), logsig, pre)
    g_ref[...] = g
    lower, upper = _tri_masks(MCH)
    lane_c = lax.broadcasted_iota(jnp.int32, (MCH, LANES), 1)
    fwd_lane = jnp.logical_and(lane_c >= ng, lane_c < ng + M_HEADS)
    for c in range(TB // MCH):
        gc = g[c * MCH:(c + 1) * MCH, :]
        pref = _tri_dot(lower.astype(BF16), gc)
        suff = _tri_dot(upper.astype(BF16), gc)
        bc_ref[c * MCH:(c + 1) * MCH, :] = jnp.where(fwd_lane, pref, suff)


def _l0_qkv(xm, conv_w, conv_b, wq, wk, wv, wg, bg):
    full = lambda shape: pl.BlockSpec(shape, lambda i: (0,) * len(shape))
    tok = lambda w: pl.BlockSpec((TB, w), lambda i: (i, 0))
    return pl.pallas_call(
        _l0_qkv_kernel,
        grid=(NB,),
        in_specs=_halo_specs(M_INNER) + [
            full((CONV_W, M_INNER)), full((1, M_INNER)),
            full((M_HEADS, M_DH, M_DH)), full((M_HEADS, M_DH, M_DH)), full((M_HEADS, M_DH, M_DH)),
            full((3 * M_INNER, LANES)), full((1, LANES)),
        ],
        out_specs=[tok(M_INNER), tok(M_INNER), tok(M_INNER), tok(M_INNER), tok(LANES), tok(LANES)],
        out_shape=[
            jax.ShapeDtypeStruct((N_TOK, M_INNER), BF16),
            jax.ShapeDtypeStruct((N_TOK, M_INNER), BF16),
            jax.ShapeDtypeStruct((N_TOK, M_INNER), BF16),
            jax.ShapeDtypeStruct((N_TOK, M_INNER), BF16),
            jax.ShapeDtypeStruct((N_TOK, LANES), F32),
            jax.ShapeDtypeStruct((N_TOK, LANES), F32),
        ],
        scratch_shapes=[pltpu.VMEM((TB + 2 * HALO, M_INNER), F32)],
        compiler_params=_cparams(("arbitrary",)),
        name="l0_qkv",
    )(xm, xm, xm, conv_w, conv_b, wq, wk, wv, wg, bg)


def _mlstm_chunk(r, q, k, v, gh, ght, c_ref, n_ref, m_ref):
    n = q.shape[0]
    lower, upper = _tri_masks(n)
    mask = lower if r == 0 else upper
    li_col, li_row = gh[:, r:r + 1], ght[r:r + 1, :]
    b_col, b_row = gh[:, 2 + r:3 + r], ght[2 + r:3 + r, :]
    m_prev = m_ref[r]
    c_prev = c_ref[r]
    n_prev = n_ref[r]
    d = jnp.where(mask, b_col - b_row + li_row, -jnp.inf)
    g = b_col + m_prev
    m_t = jnp.maximum(g, jnp.max(d, axis=1, keepdims=True))
    qk = lax.dot_general(q, k, (((1,), (1,)), ((), ())), preferred_element_type=F32)
    s = qk * jnp.exp(d - m_t)
    inter = jnp.exp(g - m_t)
    qf = q.astype(F32)
    num = _bdot(s, v) + inter * _bdot(q, c_prev)
    den = jnp.sum(s, axis=1, keepdims=True) + inter * jnp.sum(qf * n_prev, axis=1, keepdims=True)
    h = num / jnp.maximum(jnp.abs(den), jnp.exp(-m_t))
    b_last = b_col[n - 1:n, :] if r == 0 else b_col[0:1, :]
    ld = b_last - b_col + li_col
    m_new = jnp.maximum(b_last + m_prev, jnp.max(ld, axis=0, keepdims=True))
    wk = jnp.exp(ld - m_new)
    decay = jnp.exp(b_last + m_prev - m_new)
    kw = k.astype(F32) * wk
    upd = lax.dot_general(kw.astype(BF16), v, (((0,), (0,)), ((), ())), preferred_element_type=F32)
    c_ref[r] = decay * c_prev + upd
    n_ref[r] = decay * n_prev + jnp.sum(kw, axis=0, keepdims=True)
    m_ref[r] = m_new
    return h.astype(BF16)


def _mlstm_kernel(qf_ref, kf_ref, vf_ref, ghf_ref, ghtf_ref, qb_ref, kb_ref, vb_ref, ghb_ref, ghtb_ref,
                  c0_ref, n0_ref, m0_ref,
                  hf_ref, hb_ref, co_ref, no_ref, mo_ref,
                  c_ref, n_ref, m_ref):
    s = pl.program_id(1)

    @pl.when(_seq_start(s))
    def _():
        is_ctx = s < NB_CTX
        c_ref[...] = jnp.where(is_ctx, 0.0, c0_ref[...])
        n_ref[...] = jnp.where(is_ctx, 0.0, n0_ref[...])
        m_ref[...] = jnp.where(is_ctx, 0.0, m0_ref[...])

    for c in range(TB // MCH):
        rows = slice(c * MCH, (c + 1) * MCH)
        hf_ref[rows, :] = _mlstm_chunk(0, qf_ref[rows, :], kf_ref[rows, :], vf_ref[rows, :],
                                       ghf_ref[rows, :], ghtf_ref[:, rows], c_ref, n_ref, m_ref)
    for c in reversed(range(TB // MCH)):
        rows = slice(c * MCH, (c + 1) * MCH)
        hb_ref[rows, :] = _mlstm_chunk(1, qb_ref[rows, :], kb_ref[rows, :], vb_ref[rows, :],
                                       ghb_ref[rows, :], ghtb_ref[:, rows], c_ref, n_ref, m_ref)

    @pl.when(s < NB_CTX)
    def _():
        co_ref[...] = c_ref[...]
        no_ref[...] = n_ref[...]
        mo_ref[...] = m_ref[...]


def _mlstm_scan(q, k, v, gh, ght, c0, n0, m0):
    fwd = lambda h, s: (s, h)
    bwd = lambda h, s: (_bwd_block(s), h)
    tokf = pl.BlockSpec((TB, M_DH), fwd)
    tokb = pl.BlockSpec((TB, M_DH), bwd)
    ghf = pl.BlockSpec((None, TB, 4), lambda h, s: (h, s, 0))
    ghb = pl.BlockSpec((None, TB, 4), lambda h, s: (h, _bwd_block(s), 0))
    ghtf = pl.BlockSpec((None, 4, TB), lambda h, s: (h, 0, s))
    ghtb = pl.BlockSpec((None, 4, TB), lambda h, s: (h, 0, _bwd_block(s)))
    lat = _lat_seq_of_block
    ctx = lambda s: jnp.minimum(s, NB_CTX - 1)
    return pl.pallas_call(
        _mlstm_kernel,
        grid=(M_HEADS, NB),
        in_specs=[
            tokf, tokf, tokf, ghf, ghtf, tokb, tokb, tokb, ghb, ghtb,
            pl.BlockSpec((None, None, N_DIR, None, M_DH, M_DH), lambda h, s: (lat(s), 0, 0, h, 0, 0)),
            pl.BlockSpec((None, N_DIR, None, 1, M_DH), lambda h, s: (lat(s), 0, h, 0, 0)),
            pl.BlockSpec((None, N_DIR, None, 1, 1), lambda h, s: (lat(s), 0, h, 0, 0)),
        ],
        out_specs=[
            tokf, tokb,
            pl.BlockSpec((None, None, N_DIR, None, M_DH, M_DH), lambda h, s: (ctx(s), 0, 0, h, 0, 0)),
            pl.BlockSpec((None, N_DIR, None, 1, M_DH), lambda h, s: (ctx(s), 0, h, 0, 0)),
            pl.BlockSpec((None, N_DIR, None, 1, 1), lambda h, s: (ctx(s), 0, h, 0, 0)),
        ],
        out_shape=[
            jax.ShapeDtypeStruct((N_TOK, M_INNER), BF16),
            jax.ShapeDtypeStruct((N_TOK, M_INNER), BF16),
            jax.ShapeDtypeStruct((BATCH, 1, N_DIR, M_HEADS, M_DH, M_DH), F32),
            jax.ShapeDtypeStruct((BATCH, N_DIR, M_HEADS, 1, M_DH), F32),
            jax.ShapeDtypeStruct((BATCH, N_DIR, M_HEADS, 1, 1), F32),
        ],
        scratch_shapes=[
            pltpu.VMEM((N_DIR, M_DH, M_DH), F32),
            pltpu.VMEM((N_DIR, 1, M_DH), F32),
            pltpu.VMEM((N_DIR, 1, 1), F32),
        ],
        compiler_params=_cparams(("arbitrary", "arbitrary")),
        name="mlstm_scan",
    )(q, k, v, gh, ght, q, k, v, gh, ght, c0, n0, m0)


def _post_ln(x, branch, gate, ln_g, ln_b):
    return _layer_norm(ALPHA * x + gate * branch, ln_g, ln_b)


def _l0_out_kernel(hf_ref, hb_ref, xc_ref, z_ref, nw_ref, sk_ref, w_ref, xa_ref, mod_ref, lg_ref, lb_ref,
                   x1_ref, h2_ref, o_scr):
    for h in range(M_HEADS):
        c0, c1 = h * M_DH, (h + 1) * M_DH
        hs = hf_ref[:, c0:c1].astype(F32) + hb_ref[:, c0:c1].astype(F32)
        mu = jnp.mean(hs, axis=-1, keepdims=True)
        d = hs - mu
        var = jnp.mean(d * d, axis=-1, keepdims=True)
        hn = d * lax.rsqrt(var + LN_EPS) * nw_ref[:, c0:c1]
        o = (hn + sk_ref[:, c0:c1] * xc_ref[:, c0:c1].astype(F32)) * _silu(z_ref[:, c0:c1].astype(F32))
        o_scr[:, c0:c1] = o.astype(BF16)
    out = jnp.dot(o_scr[...], w_ref[...], preferred_element_type=F32)
    x1 = _post_ln(xa_ref[...], out, mod_ref[2:3, :], lg_ref[...], lb_ref[...])
    x1_ref[...] = x1
    h2_ref[...] = (x1 * (1.0 + mod_ref[4:5, :]) + mod_ref[3:4, :]).astype(BF16)


def _l0_out(hf, hb, xc, z, norm_w, skip, w_down, xa, mod, ln_g, ln_b):
    full = lambda shape: pl.BlockSpec(shape, lambda i: (0,) * len(shape))
    tok = lambda w: pl.BlockSpec((TB, w), lambda i: (i, 0))
    return pl.pallas_call(
        _l0_out_kernel,
        grid=(NB,),
        in_specs=[tok(M_INNER), tok(M_INNER), tok(M_INNER), tok(M_INNER),
                  full((1, M_INNER)), full((1, M_INNER)), full((M_INNER, D_MODEL)),
                  tok(D_MODEL), _mod_spec(0, TB), full((1, D_MODEL)), full((1, D_MODEL))],
        out_specs=[tok(D_MODEL), tok(D_MODEL)],
        out_shape=[jax.ShapeDtypeStruct((N_TOK, D_MODEL), F32),
                   jax.ShapeDtypeStruct((N_TOK, D_MODEL), BF16)],
        scratch_shapes=[pltpu.VMEM((TB, M_INNER), BF16)],
        compiler_params=_cparams(("arbitrary",)),
        name="l0_out",
    )(hf, hb, xc, z, norm_w, skip, w_down, xa, mod, ln_g, ln_b)


def _ffn_kernel(h_ref, w1_ref, w3_ref, w2_ref, x_ref, mod_ref, lg_ref, lb_ref, modn_ref,
                x2_ref, h3_ref, acc_ref):
    f = pl.program_id(1)

    @pl.when(f == 0)
    def _():
        acc_ref[...] = jnp.zeros_like(acc_ref)

    h = h_ref[...]
    a = jnp.dot(h, w1_ref[...], preferred_element_type=F32)
    b = jnp.dot(h, w3_ref[...], preferred_element_type=F32)
    acc_ref[...] += _bdot(_silu(a) * b, w2_ref[...])

    @pl.when(f == pl.num_programs(1) - 1)
    def _():
        x2 = _post_ln(x_ref[...], acc_ref[...], mod_ref[5:6, :], lg_ref[...], lb_ref[...])
        x2_ref[...] = x2
        h3_ref[...] = (x2 * (1.0 + modn_ref[1:2, :]) + modn_ref[0:1, :]).astype(BF16)


def _ffn(h2, w1, w3, w2, x1, mod, ln_g, ln_b):
    tok = lambda w: pl.BlockSpec((TM_FF, w), lambda i, f: (i, 0))
    vec = pl.BlockSpec((1, D_MODEL), lambda i, f: (0, 0))
    return pl.pallas_call(
        _ffn_kernel,
        grid=(N_TOK // TM_FF, D_FF // TF_FF),
        in_specs=[tok(D_MODEL),
                  pl.BlockSpec((D_MODEL, TF_FF), lambda i, f: (0, f)),
                  pl.BlockSpec((D_MODEL, TF_FF), lambda i, f: (0, f)),
                  pl.BlockSpec((TF_FF, D_MODEL), lambda i, f: (f, 0)),
                  tok(D_MODEL), _mod_spec(0, TM_FF), vec, vec, _mod_spec(1, TM_FF)],
        out_specs=[tok(D_MODEL), tok(D_MODEL)],
        out_shape=[jax.ShapeDtypeStruct((N_TOK, D_MODEL), F32),
                   jax.ShapeDtypeStruct((N_TOK, D_MODEL), BF16)],
        scratch_shapes=[pltpu.VMEM((TM_FF, D_MODEL), F32)],
        compiler_params=_cparams(("arbitrary", "arbitrary")),
        name="ffn",
    )(h2, w1, w3, w2, x1, mod, ln_g, ln_b, mod)


def _l1_in_kernel(h_ref, wz_ref, wx_ref, wd_ref, z_ref, xbc_ref, dt_ref):
    h = h_ref[...]
    z_ref[...] = jnp.dot(h, wz_ref[...], preferred_element_type=F32).astype(BF16)
    xbc_ref[...] = jnp.dot(h, wx_ref[...], preferred_element_type=F32)
    dt_ref[...] = jnp.dot(h, wd_ref[...], preferred_element_type=F32)


def _l1_in(h3, wz, wx, wd):
    full = lambda shape: pl.BlockSpec(shape, lambda i: (0,) * len(shape))
    tok = lambda w: pl.BlockSpec((TB, w), lambda i: (i, 0))
    return pl.pallas_call(
        _l1_in_kernel,
        grid=(NB,),
        in_specs=[tok(D_MODEL), full((D_MODEL, S_INNER)), full((D_MODEL, S_CONV_CH)), full((D_MODEL, LANES))],
        out_specs=[tok(S_INNER), tok(S_CONV_CH), tok(LANES)],
        out_shape=[jax.ShapeDtypeStruct((N_TOK, S_INNER), BF16),
                   jax.ShapeDtypeStruct((N_TOK, S_CONV_CH), F32),
                   jax.ShapeDtypeStruct((N_TOK, LANES), F32)],
        compiler_params=_cparams(("arbitrary",)),
        name="l1_in",
    )(h3, wz, wx, wd)


def _l1_conv_kernel(prev_ref, xbc_ref, next_ref, cw_ref, cb_ref, dtr_ref, dtb_ref, a_ref,
                    xs_ref, bmt_ref, cm_ref, dt_ref, cum_ref, ext_ref):
    _fill_ext(ext_ref, prev_ref, xbc_ref, next_ref)
    cw = 512
    for j in range(S_INNER // cw):
        c0, c1 = j * cw, (j + 1) * cw
        xs_ref[:, c0:c1] = _silu(_conv_cols(ext_ref, cw_ref, cb_ref, c0, c1)).astype(BF16)
    for j in range(S_GN // cw):
        c0, c1 = S_INNER + j * cw, S_INNER + (j + 1) * cw
        bm = _silu(_conv_cols(ext_ref, cw_ref, cb_ref, c0, c1))
        bmt_ref[j * cw:(j + 1) * cw, :] = bm.T.astype(BF16)
        c0, c1 = c0 + S_GN, c1 + S_GN
        cm_ref[:, j * cw:(j + 1) * cw] = _silu(_conv_cols(ext_ref, cw_ref, cb_ref, c0, c1)).astype(BF16)
    x = dtr_ref[...] + dtb_ref[...]
    dt = jnp.maximum(x, 0.0) + jnp.log1p(jnp.exp(-jnp.abs(x)))
    dt_ref[...] = dt
    da = dt * a_ref[...]
    lower, upper = _tri_masks()
    lane_c = lax.broadcasted_iota(jnp.int32, (CHUNK, LANES), 1)
    for c in range(CPB):
        dc = da[c * CHUNK:(c + 1) * CHUNK, :]
        pref = _tri_dot(lower.astype(BF16), dc)
        suff = _tri_dot(upper.astype(BF16), dc)
        cum_ref[c * CHUNK:(c + 1) * CHUNK, :] = jnp.where(lane_c < S_HEADS, pref, suff)


def _l1_conv(xbc, conv_w, conv_b, dt_raw, dt_bias, a_neg):
    full = lambda shape: pl.BlockSpec(shape, lambda i: (0,) * len(shape))
    tok = lambda w: pl.BlockSpec((TB, w), lambda i: (i, 0))
    return pl.pallas_call(
        _l1_conv_kernel,
        grid=(NB,),
        in_specs=_halo_specs(S_CONV_CH) + [
            full((CONV_W, S_CONV_CH)), full((1, S_CONV_CH)), tok(LANES), full((1, LANES)), full((1, LANES))],
        out_specs=[tok(S_INNER), pl.BlockSpec((S_GN, TB), lambda i: (0, i)), tok(S_GN), tok(LANES), tok(LANES)],
        out_shape=[jax.ShapeDtypeStruct((N_TOK, S_INNER), BF16),
                   jax.ShapeDtypeStruct((S_GN, N_TOK), BF16),
                   jax.ShapeDtypeStruct((N_TOK, S_GN), BF16),
                   jax.ShapeDtypeStruct((N_TOK, LANES), F32),
                   jax.ShapeDtypeStruct((N_TOK, LANES), F32)],
        scratch_shapes=[pltpu.VMEM((TB + 2 * HALO, S_CONV_CH), F32)],
        compiler_params=_cparams(("arbitrary",)),
        name="l1_conv",
    )(xbc, xbc, xbc, conv_w, conv_b, dt_raw, dt_bias, a_neg)


GW = S_HPG * S_HEADDIM


def _ssd_chunk(r, x, bmt, cm, dg, dgt, sel, dsk, st_ref, y_ref, rows):
    lower, upper = _tri_masks()
    mask = lower if r == 0 else upper
    x = x.astype(F32)
    st = st_ref[r]
    hi = dg.astype(BF16)
    r1 = dg - hi.astype(F32)
    mid = r1.astype(BF16)
    lo = (r1 - mid.astype(F32)).astype(BF16)
    spread = lambda a: jnp.dot(a, sel, preferred_element_type=F32)
    bro = spread(hi) + spread(mid) + spread(lo)
    cum_b = bro[:, S_HPG * CHUNK:S_HPG * CHUNK + GW]
    dt_b = bro[:, S_HPG * CHUNK + GW:S_HPG * CHUNK + 2 * GW]
    cb = jnp.dot(cm, bmt, preferred_element_type=F32)
    lane = lax.broadcasted_iota(jnp.int32, (CHUNK, GW), 1)
    ws, xd = [], []
    for j in range(S_HPG):
        c = S_HPG * r + j
        seg = bro[:, j * CHUNK:(j + 1) * CHUNK] - dgt[2 * S_HPG + c:2 * S_HPG + c + 1, :]
        w = jnp.exp(jnp.where(mask, seg, -jnp.inf)) * cb * dgt[c:c + 1, :]
        ws.append(w.astype(BF16))
        head = jnp.logical_and(lane >= j * S_HEADDIM, lane < (j + 1) * S_HEADDIM)
        xd.append(jnp.where(head, x, 0.0).astype(BF16))
    y = jnp.dot(jnp.concatenate(ws, axis=1), jnp.concatenate(xd, axis=0), preferred_element_type=F32)
    y = y + jnp.exp(cum_b) * _bdot(cm, st)
    if r == 0:
        y = y + dsk * x
    y_ref[rows, :] = y.astype(BF16)
    last_b = cum_b[CHUNK - 1:CHUNK, :] if r == 0 else cum_b[0:1, :]
    xw = x * (jnp.exp(last_b - cum_b) * dt_b)
    st_ref[r] = jnp.exp(last_b) * st + _bdot(bmt, xw)


def _ssd_kernel(xf_ref, bf_ref, cf_ref, dgf_ref, dtf_ref, xb_ref, bb_ref, cb_ref, dgb_ref, dtb_ref,
                sel_ref, dsk_ref, s0_ref, yf_ref, yb_ref, so_ref, st_ref):
    s = pl.program_id(1)

    @pl.when(_seq_start(s))
    def _():
        is_ctx = s < NB_CTX
        for r in range(N_DIR):
            s0 = jnp.concatenate([s0_ref[r, j] for j in range(S_HPG)], axis=0)
            st_ref[r] = jnp.where(is_ctx, 0.0, s0.T)

    dsk = dsk_ref[...]
    for c in range(CPB):
        rows = slice(c * CHUNK, (c + 1) * CHUNK)
        _ssd_chunk(0, xf_ref[rows, :], bf_ref[:, rows], cf_ref[rows, :], dgf_ref[rows, :], dtf_ref[:, rows],
                   sel_ref[0], dsk, st_ref, yf_ref, rows)
    for c in reversed(range(CPB)):
        rows = slice(c * CHUNK, (c + 1) * CHUNK)
        _ssd_chunk(1, xb_ref[rows, :], bb_ref[:, rows], cb_ref[rows, :], dgb_ref[rows, :], dtb_ref[:, rows],
                   sel_ref[1], dsk, st_ref, yb_ref, rows)

    @pl.when(s < NB_CTX)
    def _():
        for r in range(N_DIR):
            stt = st_ref[r].T
            for j in range(S_HPG):
                so_ref[r, j] = stt[j * S_HEADDIM:(j + 1) * S_HEADDIM, :]


def _ssd_selector():
    ng = 2 * N_DIR * S_HPG
    sel = np.zeros((N_DIR, ng, S_HPG * CHUNK + 2 * GW), np.float32)
    for r in range(N_DIR):
        for j in range(S_HPG):
            c = S_HPG * r + j
            sel[r, N_DIR * S_HPG + c, j * CHUNK:(j + 1) * CHUNK] = 1.0
            sel[r, N_DIR * S_HPG + c, S_HPG * CHUNK + j * S_HEADDIM:S_HPG * CHUNK + (j + 1) * S_HEADDIM] = 1.0
            sel[r, c, S_HPG * CHUNK + GW + j * S_HEADDIM:S_HPG * CHUNK + GW + (j + 1) * S_HEADDIM] = 1.0
    return jnp.asarray(sel, dtype=BF16)


def _ssd_scan(xs, bmt, cm, dg, dgt, dsk, s0):
    ng = 2 * N_DIR * S_HPG
    def side(blk):
        return [
            pl.BlockSpec((TB, GW), lambda g, s: (blk(s), g)),
            pl.BlockSpec((S_STATE, TB), lambda g, s: (g, blk(s))),
            pl.BlockSpec((TB, S_STATE), lambda g, s: (blk(s), g)),
            pl.BlockSpec((None, TB, ng), lambda g, s: (g, blk(s), 0)),
            pl.BlockSpec((None, ng, TB), lambda g, s: (g, 0, blk(s))),
        ]
    ident = lambda s: s
    lat = _lat_seq_of_block
    ctx = lambda s: jnp.minimum(s, NB_CTX - 1)
    st_block = (None, None, N_DIR, S_HPG, S_HEADDIM, S_STATE)
    sel = _ssd_selector()
    return pl.pallas_call(
        _ssd_kernel,
        grid=(S_GROUPS, NB),
        in_specs=side(ident) + side(_bwd_block) + [
            pl.BlockSpec(sel.shape, lambda g, s: (0, 0, 0)),
            pl.BlockSpec((None, 1, GW), lambda g, s: (g, 0, 0)),
            pl.BlockSpec(st_block, lambda g, s: (lat(s), 0, 0, g, 0, 0)),
        ],
        out_specs=[
            pl.BlockSpec((TB, GW), lambda g, s: (s, g)),
            pl.BlockSpec((TB, GW), lambda g, s: (_bwd_block(s), g)),
            pl.BlockSpec(st_block, lambda g, s: (ctx(s), 0, 0, g, 0, 0)),
        ],
        out_shape=[
            jax.ShapeDtypeStruct((N_TOK, S_INNER), BF16),
            jax.ShapeDtypeStruct((N_TOK, S_INNER), BF16),
            jax.ShapeDtypeStruct((BATCH, 1, N_DIR, S_HEADS, S_HEADDIM, S_STATE), F32),
        ],
        scratch_shapes=[pltpu.VMEM((N_DIR, S_STATE, GW), F32)],
        compiler_params=_cparams(("arbitrary", "arbitrary")),
        name="ssd_scan",
    )(xs, bmt, cm, dg, dgt, xs, bmt, cm, dg, dgt, sel, dsk, s0)


def _l1_out_kernel(yf_ref, yb_ref, z_ref, nw_ref, w_ref, x_ref, mod_ref, lg_ref, lb_ref, wr_ref,
                   x3_ref, h4_ref, gates_ref):
    y = (yf_ref[...].astype(F32) + yb_ref[...].astype(F32)) * _silu(z_ref[...].astype(F32))
    y = y * lax.rsqrt(jnp.mean(y * y, axis=-1, keepdims=True) + RMS_EPS) * nw_ref[...]
    out = _bdot(y, w_ref[...])
    x3 = _post_ln(x_ref[...], out, mod_ref[2:3, :], lg_ref[...], lb_ref[...])
    x3_ref[...] = x3
    h4 = x3 * (1.0 + mod_ref[4:5, :]) + mod_ref[3:4, :]
    for cc in range(ROW_TILE):
        h4_ref[pl.ds(cc, TB, stride=ROW_TILE), :] = h4[:, cc * LANES:(cc + 1) * LANES]
    lane = lax.broadcasted_iota(jnp.int32, (TB, LANES), 1)
    logits = jnp.full((TB, LANES), -jnp.inf, F32)
    for e in range(N_EXPERTS):
        le = jnp.sum(h4 * wr_ref[e:e + 1, :], axis=-1, keepdims=True)
        logits = jnp.where(lane == e, le, logits)
    v1 = jnp.max(logits, axis=-1, keepdims=True)
    i1 = jnp.min(jnp.where(logits == v1, lane, LANES), axis=-1, keepdims=True)
    rest = jnp.where(lane == i1, -jnp.inf, logits)
    v2 = jnp.max(rest, axis=-1, keepdims=True)
    i2 = jnp.min(jnp.where(rest == v2, lane, LANES), axis=-1, keepdims=True)
    e2 = jnp.exp(v2 - v1)
    w1 = 1.0 / (1.0 + e2)
    w2 = e2 / (1.0 + e2)
    route = jnp.where(lane == 0, w1, jnp.where(lane == 1, w2, 0.0))
    route = jnp.where(lane == 2, i1.astype(F32), jnp.where(lane == 3, i2.astype(F32), route))
    gates_ref[...] = route


def _l1_out(yf, yb, z, norm_w, w_out, x2, mod, ln_g, ln_b, wr_t):
    full = lambda shape: pl.BlockSpec(shape, lambda i: (0,) * len(shape))
    tok = lambda w: pl.BlockSpec((TB, w), lambda i: (i, 0))
    return pl.pallas_call(
        _l1_out_kernel,
        grid=(NB,),
        in_specs=[tok(S_INNER), tok(S_INNER), tok(S_INNER), full((1, S_INNER)), full((S_INNER, D_MODEL)),
                  tok(D_MODEL), _mod_spec(1, TB), full((1, D_MODEL)), full((1, D_MODEL)),
                  full((N_EXPERTS, D_MODEL))],
        out_specs=[tok(D_MODEL), pl.BlockSpec((TB * ROW_TILE, LANES), lambda i: (i, 0)), tok(LANES)],
        out_shape=[jax.ShapeDtypeStruct((N_TOK, D_MODEL), F32),
                   jax.ShapeDtypeStruct((N_TOK * ROW_TILE, LANES), F32),
                   jax.ShapeDtypeStruct((N_TOK, LANES), F32)],
        compiler_params=_cparams(("arbitrary",)),
        name="l1_out",
    )(yf, yb, z, norm_w, w_out, x2, mod, ln_g, ln_b, wr_t)


TM_E = 512
N_ASSIGN = TOP_K * N_TOK
P_ROWS = N_ASSIGN + N_EXPERTS * TM_E
N_PT = P_ROWS // TM_E
ROW_TILE = D_MODEL // LANES
NF_MOE = E_FF // TF_MOE


def _row(i):
    return pl.ds(pl.multiple_of(i * ROW_TILE, ROW_TILE), ROW_TILE)


def _expert_kernel(te_ref, nv_ref, src_ref, srcn_ref, asg_ref, asgp_ref, h_ref, w1_ref, w3_ref, w2_ref, yc_ref,
                   xbuf, xb_ref, acc_ref, ybuf, gsem, ssem):
    j, f = pl.program_id(0), pl.program_id(1)
    n_valid = nv_ref[0]
    valid = j < n_valid
    first_f, last_f = f == 0, f == NF_MOE - 1

    def gather_row(idx_ref, r):
        pltpu.make_async_copy(h_ref.at[_row(idx_ref[0, r])], xbuf.at[_row(r)], gsem).start()

    def scatter_row(idx_ref, r):
        pltpu.make_async_copy(ybuf.at[_row(r)], yc_ref.at[_row(idx_ref[0, r])], ssem).start()

    def looped(row_fn, idx_ref):
        def body(r, carry):
            row_fn(idx_ref, r)
            return carry
        lax.fori_loop(0, TM_E, body, 0, unroll=8)

    def wait_gather():
        pltpu.make_async_copy(h_ref.at[pl.ds(0, TM_E * ROW_TILE)], xbuf, gsem).wait()

    def wait_scatter():
        pltpu.make_async_copy(ybuf, yc_ref.at[pl.ds(0, TM_E * ROW_TILE)], ssem).wait()

    def load_rows():
        for cc in range(ROW_TILE):
            xb_ref[:, cc * LANES:(cc + 1) * LANES] = xbuf[pl.ds(cc, TM_E, stride=ROW_TILE), :].astype(BF16)
        acc_ref[...] = jnp.zeros_like(acc_ref)

    def swiglu_step():
        x = xb_ref[...]
        a = jnp.dot(x, w1_ref[...], preferred_element_type=F32)
        b = jnp.dot(x, w3_ref[...], preferred_element_type=F32)
        acc_ref[...] += _bdot(_silu(a) * b, w2_ref[...])

    @pl.when(jnp.logical_and(j == 0, first_f))
    def _():
        looped(gather_row, src_ref)

    @pl.when(jnp.logical_and(first_f, j <= n_valid))
    def _():
        wait_gather()

    @pl.when(jnp.logical_and(valid, jnp.logical_and(first_f, j == 0)))
    def _():
        load_rows()
        swiglu_step()

    @pl.when(jnp.logical_and(valid, jnp.logical_and(first_f, j > 0)))
    def _():
        load_rows()
        swiglu_step()
        for r in range(TM_E):
            scatter_row(asgp_ref, r)

    @pl.when(jnp.logical_and(valid, last_f))
    def _():
        for r in range(TM_E):
            gather_row(srcn_ref, r)
        swiglu_step()

    @pl.when(jnp.logical_and(valid, last_f))
    def _():
        @pl.when(j > 0)
        def _():
            wait_scatter()

        for cc in range(ROW_TILE):
            ybuf[pl.ds(cc, TM_E, stride=ROW_TILE), :] = acc_ref[:, cc * LANES:(cc + 1) * LANES]

    @pl.when(jnp.logical_and(first_f, j == n_valid))
    def _():
        looped(scatter_row, asgp_ref)
        wait_scatter()

    @pl.when(jnp.logical_and(jnp.logical_not(valid), last_f))
    def _():
        ybuf[...] = jnp.zeros_like(ybuf)
        looped(scatter_row, asg_ref)
        wait_scatter()


def _experts(tile_expert, n_valid, src, asg, h4, w1, w3, w2):
    assert NF_MOE >= 2
    fidx = lambda j, f, nv: jnp.where(j < nv[0], f, NF_MOE - 1)
    smem_tile = lambda off: pl.BlockSpec(
        (None, 1, TM_E), lambda j, f, te, nv: (jnp.clip(j + off, 0, N_PT - 1), 0, 0), memory_space=pltpu.SMEM)
    src3 = src.reshape(N_PT, 1, TM_E)
    asg3 = asg.reshape(N_PT, 1, TM_E)
    return pl.pallas_call(
        _expert_kernel,
        grid_spec=pltpu.PrefetchScalarGridSpec(
            num_scalar_prefetch=2,
            grid=(N_PT, NF_MOE),
            in_specs=[smem_tile(0), smem_tile(1), smem_tile(0), smem_tile(-1),
                      pl.BlockSpec(memory_space=pl.ANY),
                      pl.BlockSpec((None, D_MODEL, TF_MOE), lambda j, f, te, nv: (te[j], 0, fidx(j, f, nv))),
                      pl.BlockSpec((None, D_MODEL, TF_MOE), lambda j, f, te, nv: (te[j], 0, fidx(j, f, nv))),
                      pl.BlockSpec((None, TF_MOE, D_MODEL), lambda j, f, te, nv: (te[j], fidx(j, f, nv), 0))],
            out_specs=pl.BlockSpec(memory_space=pl.ANY),
            scratch_shapes=[pltpu.VMEM((TM_E * ROW_TILE, LANES), F32),
                            pltpu.VMEM((TM_E, D_MODEL), BF16),
                            pltpu.VMEM((TM_E, D_MODEL), F32),
                            pltpu.VMEM((TM_E * ROW_TILE, LANES), F32),
                            pltpu.SemaphoreType.DMA(()),
                            pltpu.SemaphoreType.DMA(())],
        ),
        out_shape=jax.ShapeDtypeStruct((P_ROWS * ROW_TILE, LANES), F32),
        compiler_params=_cparams(("arbitrary", "arbitrary")),
        name="experts",
    )(tile_expert, n_valid, src3, src3, asg3, asg3, h4, w1, w3, w2)


def _moe_out_kernel(yc_ref, r_ref, x_ref, mod_ref, lg_ref, lb_ref, yp_ref, ys_ref, f_ref):
    i = pl.program_id(0)
    w1, w2 = r_ref[:, 0:1], r_ref[:, 1:2]
    for cc in range(ROW_TILE):
        y_top1 = yc_ref[pl.ds(cc, TM_FF, stride=TOP_K * ROW_TILE), :]
        y_top2 = yc_ref[pl.ds(ROW_TILE + cc, TM_FF, stride=TOP_K * ROW_TILE), :]
        f_ref[:, cc * LANES:(cc + 1) * LANES] = w1 * y_top1 + w2 * y_top2
    y = _post_ln(x_ref[...], f_ref[...], mod_ref[5:6, :], lg_ref[...], lb_ref[...])

    @pl.when(i < N_CTX // TM_FF)
    def _():
        yp_ref[...] = y

    @pl.when(i >= N_CTX // TM_FF)
    def _():
        ys_ref[...] = y


def _moe_out(yc, route, x3, mod, ln_g, ln_b):
    nctx = N_CTX // TM_FF
    tok = lambda w: pl.BlockSpec((TM_FF, w), lambda i: (i, 0))
    vec = pl.BlockSpec((1, D_MODEL), lambda i: (0, 0))
    return pl.pallas_call(
        _moe_out_kernel,
        grid=(N_TOK // TM_FF,),
        in_specs=[pl.BlockSpec((TM_FF * TOP_K * ROW_TILE, LANES), lambda i: (i, 0)),
                  tok(LANES), tok(D_MODEL), _mod_spec(1, TM_FF), vec, vec],
        out_specs=[pl.BlockSpec((TM_FF, D_MODEL), lambda i: (jnp.minimum(i, nctx - 1), 0)),
                   pl.BlockSpec((TM_FF, D_MODEL), lambda i: (jnp.maximum(i - nctx, 0), 0))],
        out_shape=[jax.ShapeDtypeStruct((N_CTX, D_MODEL), F32),
                   jax.ShapeDtypeStruct((N_LAT, D_MODEL), F32)],
        scratch_shapes=[pltpu.VMEM((TM_FF, D_MODEL), F32)],
        compiler_params=_cparams(("arbitrary",)),
        name="moe_out",
    )(yc, route, x3, mod, ln_g, ln_b)


def _routing_tables(route):
    flat_e = route[:, 2:2 + TOP_K].astype(jnp.int32).reshape(N_ASSIGN)
    onehot = (flat_e[:, None] == jnp.arange(N_EXPERTS, dtype=jnp.int32)[None, :]).astype(jnp.int32)
    csum = jnp.cumsum(onehot, axis=0)
    rank = jnp.take_along_axis(csum, flat_e[:, None], axis=1)[:, 0] - 1
    ntile = (csum[-1] + TM_E - 1) // TM_E
    tile_end = jnp.cumsum(ntile)
    dest = (tile_end - ntile)[flat_e] * TM_E + rank
    asg = jnp.full((P_ROWS,), -1, jnp.int32).at[dest].set(jnp.arange(N_ASSIGN, dtype=jnp.int32))
    is_pad = asg < 0
    src = jnp.maximum(asg, 0) // TOP_K
    asg = jnp.where(is_pad, N_ASSIGN - 1 + jnp.cumsum(is_pad.astype(jnp.int32)), asg)
    tiles = jnp.arange(N_PT, dtype=jnp.int32)
    n_valid = tile_end[-1]
    tile_expert = jnp.sum((jnp.minimum(tiles, n_valid - 1)[:, None] >= tile_end[None, :]).astype(jnp.int32), axis=1)
    return src, asg, tile_expert, n_valid.reshape(1)


def _grid_pos_embed():
    rows = DEC_SEQ // GRID_W
    quarter = D_MODEL // 4
    freqs = jnp.exp(-math.log(POS_BASE) * jnp.arange(quarter, dtype=F32) / quarter)
    er = jnp.arange(rows, dtype=F32)[:, None] * freqs
    ec = jnp.arange(GRID_W, dtype=F32)[:, None] * freqs
    row_emb = jnp.concatenate([jnp.sin(er), jnp.cos(er)], axis=-1)
    col_emb = jnp.concatenate([jnp.sin(ec), jnp.cos(ec)], axis=-1)
    pos = jnp.concatenate([jnp.broadcast_to(row_emb[:, None, :], (rows, GRID_W, D_MODEL // 2)),
                           jnp.broadcast_to(col_emb[None, :, :], (rows, GRID_W, D_MODEL // 2))], axis=-1)
    return pos.reshape(DEC_SEQ, D_MODEL)


def _pad_lanes(a, width=LANES):
    return jnp.pad(a, [(0, 0)] * (a.ndim - 1) + [(0, width - a.shape[-1])])


def kernel(x_prompt, x_sample, c, state_mlstm_C, state_mlstm_n, state_mlstm_m, state_ssm, c_ctx,
           ada_w, ada_b, ln_g, ln_b,
           ml_w_up, ml_conv_w, ml_conv_b, ml_wq, ml_wk, ml_wv, ml_w_ig, ml_b_ig, ml_w_fg, ml_b_fg,
           ml_norm_w, ml_skip, ml_w_down,
           ss_w_in, ss_conv_w, ss_conv_b, ss_dt_bias, ss_a_log, ss_d, ss_norm_w, ss_w_out,
           ff_w1, ff_w3, ff_w2,
           moe_w_router, moe_w1, moe_w3, moe_w2):
    row = lambda a: a.reshape(1, -1)
    cond = jnp.concatenate([c_ctx[None, :], c, jnp.zeros((COND_ROWS - N_COND, D_MODEL), F32)], axis=0)
    mod = _ada_mod(cond, ada_w.astype(BF16), ada_b)

    xa, xm, z0 = _l0_in(x_prompt.reshape(N_CTX, D_MODEL), x_sample.reshape(N_LAT, D_MODEL), _grid_pos_embed(),
                        mod, ml_w_up[0].astype(BF16))
    ng = N_DIR * M_HEADS
    wg = jnp.concatenate([ml_w_ig[0].transpose(1, 0, 2).reshape(3 * M_INNER, ng),
                          ml_w_fg[0].transpose(1, 0, 2).reshape(3 * M_INNER, ng)], axis=1)
    bg = jnp.concatenate([ml_b_ig[0].reshape(ng), ml_b_fg[0].reshape(ng)])
    xc, q, k, v, g, bc = _l0_qkv(xm, ml_conv_w[0], row(ml_conv_b[0]),
                                 ml_wq[0].astype(BF16), ml_wk[0].astype(BF16), ml_wv[0].astype(BF16),
                                 _pad_lanes(wg).astype(BF16), row(_pad_lanes(bg)))
    gh = jnp.concatenate([g[:, :ng].reshape(N_TOK, N_DIR, M_HEADS),
                          bc[:, ng:2 * ng].reshape(N_TOK, N_DIR, M_HEADS)], axis=1)
    ght = gh.transpose(2, 1, 0)
    gh = gh.transpose(2, 0, 1)
    hf, hb, new_c, new_n, new_m = _mlstm_scan(
        q, k, v, gh, ght, state_mlstm_C,
        state_mlstm_n.reshape(DEC_BATCH, N_DIR, M_HEADS, 1, M_DH),
        state_mlstm_m.reshape(DEC_BATCH, N_DIR, M_HEADS, 1, 1))
    x1, h2 = _l0_out(hf, hb, xc, z0, row(ml_norm_w[0]), row(ml_skip[0]), ml_w_down[0].astype(BF16),
                     xa, mod, row(ln_g[0, 0]), row(ln_b[0, 0]))
    x2, h3 = _ffn(h2, ff_w1[0].astype(BF16), ff_w3[0].astype(BF16), ff_w2[0].astype(BF16),
                  x1, mod, row(ln_g[0, 1]), row(ln_b[0, 1]))

    w_in = ss_w_in[0]
    z1, xbc, dt_raw = _l1_in(h3, w_in[:, :S_INNER].astype(BF16),
                             w_in[:, S_INNER:S_INNER + S_CONV_CH].astype(BF16),
                             _pad_lanes(w_in[:, S_INNER + S_CONV_CH:]).astype(BF16))
    a_neg = -jnp.exp(ss_a_log[0].reshape(N_DIR * S_HEADS))
    xs, bmt, cm, dt, cum = _l1_conv(xbc, ss_conv_w[0], row(ss_conv_b[0]), dt_raw,
                                    row(_pad_lanes(ss_dt_bias[0].reshape(N_DIR * S_HEADS))), row(_pad_lanes(a_neg)))
    nh = N_DIR * S_HEADS
    to_group = lambda a: a[:, :nh].reshape(N_TOK, N_DIR, S_GROUPS, S_HPG).transpose(2, 0, 1, 3).reshape(
        S_GROUPS, N_TOK, N_DIR * S_HPG)
    dg = jnp.concatenate([to_group(dt), to_group(cum)], axis=-1)
    dgt = dg.transpose(0, 2, 1)
    dsk = jnp.repeat(ss_d[0], S_HEADDIM).reshape(S_GROUPS, 1, GW)
    yf, yb, new_s = _ssd_scan(xs, bmt, cm, dg, dgt, dsk, state_ssm)
    x3, h4, route = _l1_out(yf, yb, z1, row(ss_norm_w[0]), ss_w_out[0].astype(BF16), x2, mod,
                            row(ln_g[1, 0]), row(ln_b[1, 0]), moe_w_router[0].T)
    src, asg, tile_expert, n_valid = _routing_tables(route)
    yc = _experts(tile_expert, n_valid, src, asg, h4,
                  moe_w1[0].astype(BF16), moe_w3[0].astype(BF16), moe_w2[0].astype(BF16))
    yp, ys = _moe_out(yc, route, x3, mod, row(ln_g[1, 1]), row(ln_b[1, 1]))

    return (yp.reshape(BATCH, SEQ, D_MODEL), ys.reshape(DEC_BATCH, DEC_SEQ, D_MODEL),
            new_c,
            new_n.reshape(BATCH, 1, N_DIR, M_HEADS, M_DH),
            new_m.reshape(BATCH, 1, N_DIR, M_HEADS),
            new_s)
```

```python
import functools
import math

import jax
import jax.numpy as jnp
from jax import lax
from jax.experimental import pallas as pl
from jax.experimental.pallas import tpu as pltpu

F32 = jnp.float32
BF16 = jnp.bfloat16

D_MODEL = 1024
BATCH = 16
SEQ = 256
DEC_BATCH = 4
DEC_SEQ = 4096
GRID_W = 64
CHUNK = 128
CONV_W = 5
N_DIR = 2
M_INNER = 2 * D_MODEL
M_HEADS = 4
M_DH = M_INNER // M_HEADS
S_INNER = 2 * D_MODEL
S_HEADDIM = 64
S_HEADS = S_INNER // S_HEADDIM
S_GROUPS = 8
S_HPG = S_HEADS // S_GROUPS
S_STATE = 128
S_GN = S_GROUPS * S_STATE
S_CONV_CH = S_INNER + 2 * S_GN
D_FF = 2816
N_EXPERTS = 8
TOP_K = 2
E_FF = 3584
DEPTH = 2
ALPHA = (2 * DEPTH) ** 0.25
LN_EPS = 1e-5
RMS_EPS = 1e-5
POS_BASE = 10000.0

N_CTX = BATCH * SEQ
N_LAT = DEC_BATCH * DEC_SEQ
N_TOK = N_CTX + N_LAT
N_COND = 1 + DEC_BATCH
COND_ROWS = 8

TB = 256
NB = N_TOK // TB
NB_CTX = N_CTX // TB
NB_SEQ = DEC_SEQ // TB
HALO = 8
CPB = TB // CHUNK
MCH = TB

TM_FF = 512
TF_FF = 1408
TF_MOE = 1792
LANES = 128

VMEM_LIMIT = 56 * 1024 * 1024


def _cparams(sem):
    return pltpu.CompilerParams(dimension_semantics=sem, vmem_limit_bytes=VMEM_LIMIT)


def _cond_of_token(t0):
    return jnp.where(t0 < N_CTX, 0, 1 + (t0 - N_CTX) // DEC_SEQ)


def _lat_seq_of_block(s):
    return jnp.maximum(s - NB_CTX, 0) // NB_SEQ


def _bwd_block(s):
    k = s - NB_CTX
    return jnp.where(s < NB_CTX, s, NB_CTX + (k // NB_SEQ) * NB_SEQ + (NB_SEQ - 1 - k % NB_SEQ))


def _seq_start(s):
    return jnp.logical_or(s < NB_CTX, (s - NB_CTX) % NB_SEQ == 0)


def _seq_end(s):
    return jnp.logical_or(s < NB_CTX, (s - NB_CTX) % NB_SEQ == NB_SEQ - 1)


def _silu(x):
    return x * jax.nn.sigmoid(x)


def _bdot(a, b):
    return jnp.dot(a.astype(BF16), b.astype(BF16), preferred_element_type=F32)


def _tri_dot(tri, x):
    hi = x.astype(BF16)
    r1 = x - hi.astype(F32)
    mid = r1.astype(BF16)
    lo = (r1 - mid.astype(F32)).astype(BF16)
    d = lambda a: jnp.dot(tri, a, preferred_element_type=F32)
    return d(hi) + d(mid) + d(lo)


def _tri_masks(n=CHUNK):
    t = lax.broadcasted_iota(jnp.int32, (n, n), 0)
    u = lax.broadcasted_iota(jnp.int32, (n, n), 1)
    return u <= t, u >= t


def _layer_norm(r, g, b):
    mu = jnp.mean(r, axis=-1, keepdims=True)
    d = r - mu
    var = jnp.mean(d * d, axis=-1, keepdims=True)
    return d * lax.rsqrt(var + LN_EPS) * g + b


def _ada_kernel(cond_ref, w_ref, b_ref, o_ref):
    o_ref[...] = _bdot(_silu(cond_ref[...]), w_ref[...]) + b_ref[...]


def _ada_mod(cond, ada_w, ada_b):
    ncol = 6 * D_MODEL // D_MODEL
    out = pl.pallas_call(
        _ada_kernel,
        grid=(DEPTH, ncol),
        in_specs=[
            pl.BlockSpec((COND_ROWS, D_MODEL), lambda l, j: (0, 0)),
            pl.BlockSpec((None, D_MODEL, D_MODEL), lambda l, j: (l, 0, j)),
            pl.BlockSpec((None, 1, D_MODEL), lambda l, j: (l, 0, j)),
        ],
        out_specs=pl.BlockSpec((None, COND_ROWS, D_MODEL), lambda l, j: (l, 0, j)),
        out_shape=jax.ShapeDtypeStruct((DEPTH, COND_ROWS, 6 * D_MODEL), F32),
        compiler_params=_cparams(("arbitrary", "arbitrary")),
        name="ada_mod",
    )(cond, ada_w, ada_b.reshape(DEPTH, 1, 6 * D_MODEL))
    return out.reshape(DEPTH, COND_ROWS, 6, D_MODEL)


def _mod_spec(layer, tm):
    return pl.BlockSpec((None, None, 6, D_MODEL), lambda i, *_: (layer, _cond_of_token(i * tm), 0, 0))


def _l0_in_kernel(xp_ref, xs_ref, pos_ref, mod_ref, w_ref, xa_ref, xm_ref, z_ref):
    i = pl.program_id(0)
    x = jnp.where(i >= NB_CTX, xs_ref[...] + pos_ref[...], xp_ref[...])
    xa_ref[...] = x
    h = x * (1.0 + mod_ref[1:2, :]) + mod_ref[0:1, :]
    u = _bdot(h, w_ref[...])
    xm_ref[...] = u[:, :M_INNER]
    z_ref[...] = u[:, M_INNER:].astype(BF16)


def _l0_in(xp, xs, pos, mod, w_up):
    return pl.pallas_call(
        _l0_in_kernel,
        grid=(NB,),
        in_specs=[
            pl.BlockSpec((TB, D_MODEL), lambda i: (jnp.minimum(i, NB_CTX - 1), 0)),
            pl.BlockSpec((TB, D_MODEL), lambda i: (jnp.maximum(i - NB_CTX, 0), 0)),
            pl.BlockSpec((TB, D_MODEL), lambda i: (jnp.maximum(i - NB_CTX, 0) % NB_SEQ, 0)),
            _mod_spec(0, TB),
            pl.BlockSpec((D_MODEL, 2 * M_INNER), lambda i: (0, 0)),
        ],
        out_specs=[
            pl.BlockSpec((TB, D_MODEL), lambda i: (i, 0)),
            pl.BlockSpec((TB, M_INNER), lambda i: (i, 0)),
            pl.BlockSpec((TB, M_INNER), lambda i: (i, 0)),
        ],
        out_shape=[
            jax.ShapeDtypeStruct((N_TOK, D_MODEL), F32),
            jax.ShapeDtypeStruct((N_TOK, M_INNER), F32),
            jax.ShapeDtypeStruct((N_TOK, M_INNER), BF16),
        ],
        compiler_params=_cparams(("arbitrary",)),
        name="l0_in",
    )(xp, xs, pos, mod, w_up)


def _fill_ext(ext_ref, prev_ref, main_ref, next_ref):
    i = pl.program_id(0)
    ext_ref[0:HALO, :] = jnp.where(_seq_start(i), 0.0, prev_ref[...])
    ext_ref[HALO:HALO + TB, :] = main_ref[...]
    ext_ref[HALO + TB:HALO + TB + HALO, :] = jnp.where(_seq_end(i), 0.0, next_ref[...])


def _conv_cols(ext_ref, w_ref, b_ref, c0, c1):
    acc = b_ref[:, c0:c1]
    for j in range(CONV_W):
        r0 = HALO - CONV_W // 2 + j
        acc = acc + w_ref[j:j + 1, c0:c1] * ext_ref[r0:r0 + TB, c0:c1]
    return acc


def _halo_specs(width):
    nh = TB // HALO
    last = N_TOK // HALO - 1
    return [
        pl.BlockSpec((HALO, width), lambda i: (jnp.maximum(i * nh - 1, 0), 0)),
        pl.BlockSpec((TB, width), lambda i: (i, 0)),
        pl.BlockSpec((HALO, width), lambda i: (jnp.minimum((i + 1) * nh, last), 0)),
    ]


def _l0_qkv_kernel(prev_ref, xm_ref, next_ref, cw_ref, cb_ref, wq_ref, wk_ref, wv_ref, wg_ref, bg_ref,
                   xc_ref, q_ref, k_ref, v_ref, g_ref, bc_ref, ext_ref):
    _fill_ext(ext_ref, prev_ref, xm_ref, next_ref)
    pre = jnp.zeros((TB, LANES), F32) + bg_ref[...]
    for h in range(M_HEADS):
        c0, c1 = h * M_DH, (h + 1) * M_DH
        xc = _silu(_conv_cols(ext_ref, cw_ref, cb_ref, c0, c1))
        xcb = xc.astype(BF16)
        xc_ref[:, c0:c1] = xcb
        q = jnp.dot(xcb, wq_ref[h], preferred_element_type=F32).astype(BF16)
        k = (jnp.dot(xcb, wk_ref[h], preferred_element_type=F32) * (M_DH ** -0.5)).astype(BF16)
        v = jnp.dot(xm_ref[:, c0:c1].astype(BF16), wv_ref[h], preferred_element_type=F32).astype(BF16)
        q_ref[:, c0:c1] = q
        k_ref[:, c0:c1] = k
        v_ref[:, c0:c1] = v
        pre = pre + jnp.dot(q, wg_ref[c0:c1, :], preferred_element_type=F32)
        pre = pre + jnp.dot(k, wg_ref[M_INNER + c0:M_INNER + c1, :], preferred_element_type=F32)
        pre = pre + jnp.dot(v, wg_ref[2 * M_INNER + c0:2 * M_INNER + c1, :], preferred_element_type=F32)
    lane = lax.broadcasted_iota(jnp.int32, (TB, LANES), 1)
    ng = N_DIR * M_HEADS
    logsig = -(jnp.maximum(-pre, 0.0) + jnp.log1p(jnp.exp(-jnp.abs(pre))))
    g = jnp.where(jnp.logical_and(lane >= ng, lane < 2 * ng), logsig, pre)
    g_ref[...] = g
    lower, upper = _tri_masks(MCH)
    lane_c = lax.broadcasted_iota(jnp.int32, (MCH, LANES), 1)
    fwd_lane = jnp.logical_and(lane_c >= ng, lane_c < ng + M_HEADS)
    for c in range(TB // MCH):
        gc = g[c * MCH:(c + 1) * MCH, :]
        pref = _tri_dot(lower.astype(BF16), gc)
        suff = _tri_dot(upper.astype(BF16), gc)
        bc_ref[c * MCH:(c + 1) * MCH, :] = jnp.where(fwd_lane, pref, suff)


def _l0_qkv(xm, conv_w, conv_b, wq, wk, wv, wg, bg):
    full = lambda shape: pl.BlockSpec(shape, lambda i: (0,) * len(shape))
    tok = lambda w: pl.BlockSpec((TB, w), lambda i: (i, 0))
    return pl.pallas_call(
        _l0_qkv_kernel,
        grid=(NB,),
        in_specs=_halo_specs(M_INNER) + [
            full((CONV_W, M_INNER)), full((1, M_INNER)),
            full((M_HEADS, M_DH, M_DH)), full((M_HEADS, M_DH, M_DH)), full((M_HEADS, M_DH, M_DH)),
            full((3 * M_INNER, LANES)), full((1, LANES)),
        ],
        out_specs=[tok(M_INNER), tok(M_INNER), tok(M_INNER), tok(M_INNER), tok(LANES), tok(LANES)],
        out_shape=[
            jax.ShapeDtypeStruct((N_TOK, M_INNER), BF16),
            jax.ShapeDtypeStruct((N_TOK, M_INNER), BF16),
            jax.ShapeDtypeStruct((N_TOK, M_INNER), BF16),
            jax.ShapeDtypeStruct((N_TOK, M_INNER), BF16),
            jax.ShapeDtypeStruct((N_TOK, LANES), F32),
            jax.ShapeDtypeStruct((N_TOK, LANES), F32),
        ],
        scratch_shapes=[pltpu.VMEM((TB + 2 * HALO, M_INNER), F32)],
        compiler_params=_cparams(("arbitrary",)),
        name="l0_qkv",
    )(xm, xm, xm, conv_w, conv_b, wq, wk, wv, wg, bg)


def _mlstm_chunk(r, q, k, v, gh, ght, c_ref, n_ref, m_ref):
    n = q.shape[0]
    lower, upper = _tri_masks(n)
    mask = lower if r == 0 else upper
    li_col, li_row = gh[:, r:r + 1], ght[r:r + 1, :]
    b_col, b_row = gh[:, 2 + r:3 + r], ght[2 + r:3 + r, :]
    m_prev = m_ref[r]
    c_prev = c_ref[r]
    n_prev = n_ref[r]
    d = jnp.where(mask, b_col - b_row + li_row, -jnp.inf)
    g = b_col + m_prev
    m_t = jnp.maximum(g, jnp.max(d, axis=1, keepdims=True))
    qk = lax.dot_general(q, k, (((1,), (1,)), ((), ())), preferred_element_type=F32)
    s = qk * jnp.exp(d - m_t)
    inter = jnp.exp(g - m_t)
    qf = q.astype(F32)
    num = _bdot(s, v) + inter * _bdot(q, c_prev)
    den = jnp.sum(s, axis=1, keepdims=True) + inter * jnp.sum(qf * n_prev, axis=1, keepdims=True)
    h = num / jnp.maximum(jnp.abs(den), jnp.exp(-m_t))
    b_last = b_col[n - 1:n, :] if r == 0 else b_col[0:1, :]
    ld = b_last - b_col + li_col
    m_new = jnp.maximum(b_last + m_prev, jnp.max(ld, axis=0, keepdims=True))
    wk = jnp.exp(ld - m_new)
    decay = jnp.exp(b_last + m_prev - m_new)
    kw = k.astype(F32) * wk
    upd = lax.dot_general(kw.astype(BF16), v, (((0,), (0,)), ((), ())), preferred_element_type=F32)
    c_ref[r] = decay * c_prev + upd
    n_ref[r] = decay * n_prev + jnp.sum(kw, axis=0, keepdims=True)
    m_ref[r] = m_new
    return h.astype(BF16)


def _mlstm_kernel(qf_ref, kf_ref, vf_ref, ghf_ref, ghtf_ref, qb_ref, kb_ref, vb_ref, ghb_ref, ghtb_ref,
                  c0_ref, n0_ref, m0_ref,
                  hf_ref, hb_ref, co_ref, no_ref, mo_ref,
                  c_ref, n_ref, m_ref):
    s = pl.program_id(1)

    @pl.when(_seq_start(s))
    def _():
        is_ctx = s < NB_CTX
        c_ref[...] = jnp.where(is_ctx, 0.0, c0_ref[...])
        n_ref[...] = jnp.where(is_ctx, 0.0, n0_ref[...])
        m_ref[...] = jnp.where(is_ctx, 0.0, m0_ref[...])

    for c in range(TB // MCH):
        rows = slice(c * MCH, (c + 1) * MCH)
        hf_ref[rows, :] = _mlstm_chunk(0, qf_ref[rows, :], kf_ref[rows, :], vf_ref[rows, :],
                                       ghf_ref[rows, :], ghtf_ref[:, rows], c_ref, n_ref, m_ref)
    for c in reversed(range(TB // MCH)):
        rows = slice(c * MCH, (c + 1) * MCH)
        hb_ref[rows, :] = _mlstm_chunk(1, qb_ref[rows, :], kb_ref[rows, :], vb_ref[rows, :],
                                       ghb_ref[rows, :], ghtb_ref[:, rows], c_ref, n_ref, m_ref)

    @pl.when(s < NB_CTX)
    def _():
        co_ref[...] = c_ref[...]
        no_ref[...] = n_ref[...]
        mo_ref[...] = m_ref[...]


def _mlstm_scan(q, k, v, gh, ght, c0, n0, m0):
    fwd = lambda h, s: (s, h)
    bwd = lambda h, s: (_bwd_block(s), h)
    tokf = pl.BlockSpec((TB, M_DH), fwd)
    tokb = pl.BlockSpec((TB, M_DH), bwd)
    ghf = pl.BlockSpec((None, TB, 4), lambda h, s: (h, s, 0))
    ghb = pl.BlockSpec((None, TB, 4), lambda h, s: (h, _bwd_block(s), 0))
    ghtf = pl.BlockSpec((None, 4, TB), lambda h, s: (h, 0, s))
    ghtb = pl.BlockSpec((None, 4, TB), lambda h, s: (h, 0, _bwd_block(s)))
    lat = _lat_seq_of_block
    ctx = lambda s: jnp.minimum(s, NB_CTX - 1)
    return pl.pallas_call(
        _mlstm_kernel,
        grid=(M_HEADS, NB),
        in_specs=[
            tokf, tokf, tokf, ghf, ghtf, tokb, tokb, tokb, ghb, ghtb,
            pl.BlockSpec((None, None, N_DIR, None, M_DH, M_DH), lambda h, s: (lat(s), 0, 0, h, 0, 0)),
            pl.BlockSpec((None, N_DIR, None, 1, M_DH), lambda h, s: (lat(s), 0, h, 0, 0)),
            pl.BlockSpec((None, N_DIR, None, 1, 1), lambda h, s: (lat(s), 0, h, 0, 0)),
        ],
        out_specs=[
            tokf, tokb,
            pl.BlockSpec((None, None, N_DIR, None, M_DH, M_DH), lambda h, s: (ctx(s), 0, 0, h, 0, 0)),
            pl.BlockSpec((None, N_DIR, None, 1, M_DH), lambda h, s: (ctx(s), 0, h, 0, 0)),
            pl.BlockSpec((None, N_DIR, None, 1, 1), lambda h, s: (ctx(s), 0, h, 0, 0)),
        ],
        out_shape=[
            jax.ShapeDtypeStruct((N_TOK, M_INNER), BF16),
            jax.ShapeDtypeStruct((N_TOK, M_INNER), BF16),
            jax.ShapeDtypeStruct((BATCH, 1, N_DIR, M_HEADS, M_DH, M_DH), F32),
            jax.ShapeDtypeStruct((BATCH, N_DIR, M_HEADS, 1, M_DH), F32),
            jax.ShapeDtypeStruct((BATCH, N_DIR, M_HEADS, 1, 1), F32),
        ],
        scratch_shapes=[
            pltpu.VMEM((N_DIR, M_DH, M_DH), F32),
            pltpu.VMEM((N_DIR, 1, M_DH), F32),
            pltpu.VMEM((N_DIR, 1, 1), F32),
        ],
        compiler_params=_cparams(("arbitrary", "arbitrary")),
        name="mlstm_scan",
    )(q, k, v, gh, ght, q, k, v, gh, ght, c0, n0, m0)


def _post_ln(x, branch, gate, ln_g, ln_b):
    return _layer_norm(ALPHA * x + gate * branch, ln_g, ln_b)


def _l0_out_kernel(hf_ref, hb_ref, xc_ref, z_ref, nw_ref, sk_ref, w_ref, xa_ref, mod_ref, lg_ref, lb_ref,
                   x1_ref, h2_ref, o_scr):
    for h in range(M_HEADS):
        c0, c1 = h * M_DH, (h + 1) * M_DH
        hs = hf_ref[:, c0:c1].astype(F32) + hb_ref[:, c0:c1].astype(F32)
        mu = jnp.mean(hs, axis=-1, keepdims=True)
        d = hs - mu
        var = jnp.mean(d * d, axis=-1, keepdims=True)
        hn = d * lax.rsqrt(var + LN_EPS) * nw_ref[:, c0:c1]
        o = (hn + sk_ref[:, c0:c1] * xc_ref[:, c0:c1].astype(F32)) * _silu(z_ref[:, c0:c1].astype(F32))
        o_scr[:, c0:c1] = o.astype(BF16)
    out = jnp.dot(o_scr[...], w_ref[...], preferred_element_type=F32)
    x1 = _post_ln(xa_ref[...], out, mod_ref[2:3, :], lg_ref[...], lb_ref[...])
    x1_ref[...] = x1
    h2_ref[...] = (x1 * (1.0 + mod_ref[4:5, :]) + mod_ref[3:4, :]).astype(BF16)


def _l0_out(hf, hb, xc, z, norm_w, skip, w_down, xa, mod, ln_g, ln_b):
    full = lambda shape: pl.BlockSpec(shape, lambda i: (0,) * len(shape))
    tok = lambda w: pl.BlockSpec((TB, w), lambda i: (i, 0))
    return pl.pallas_call(
        _l0_out_kernel,
        grid=(NB,),
        in_specs=[tok(M_INNER), tok(M_INNER), tok(M_INNER), tok(M_INNER),
                  full((1, M_INNER)), full((1, M_INNER)), full((M_INNER, D_MODEL)),
                  tok(D_MODEL), _mod_spec(0, TB), full((1, D_MODEL)), full((1, D_MODEL))],
        out_specs=[tok(D_MODEL), tok(D_MODEL)],
        out_shape=[jax.ShapeDtypeStruct((N_TOK, D_MODEL), F32),
                   jax.ShapeDtypeStruct((N_TOK, D_MODEL), BF16)],
        scratch_shapes=[pltpu.VMEM((TB, M_INNER), BF16)],
        compiler_params=_cparams(("arbitrary",)),
        name="l0_out",
    )(hf, hb, xc, z, norm_w, skip, w_down, xa, mod, ln_g, ln_b)


def _ffn_kernel(h_ref, w1_ref, w3_ref, w2_ref, x_ref, mod_ref, lg_ref, lb_ref, modn_ref,
                x2_ref, h3_ref, acc_ref):
    f = pl.program_id(1)

    @pl.when(f == 0)
    def _():
        acc_ref[...] = jnp.zeros_like(acc_ref)

    h = h_ref[...]
    a = jnp.dot(h, w1_ref[...], preferred_element_type=F32)
    b = jnp.dot(h, w3_ref[...], preferred_element_type=F32)
    acc_ref[...] += _bdot(_silu(a) * b, w2_ref[...])

    @pl.when(f == pl.num_programs(1) - 1)
    def _():
        x2 = _post_ln(x_ref[...], acc_ref[...], mod_ref[5:6, :], lg_ref[...], lb_ref[...])
        x2_ref[...] = x2
        h3_ref[...] = (x2 * (1.0 + modn_ref[1:2, :]) + modn_ref[0:1, :]).astype(BF16)


def _ffn(h2, w1, w3, w2, x1, mod, ln_g, ln_b):
    tok = lambda w: pl.BlockSpec((TM_FF, w), lambda i, f: (i, 0))
    vec = pl.BlockSpec((1, D_MODEL), lambda i, f: (0, 0))
    return pl.pallas_call(
        _ffn_kernel,
        grid=(N_TOK // TM_FF, D_FF // TF_FF),
        in_specs=[tok(D_MODEL),
                  pl.BlockSpec((D_MODEL, TF_FF), lambda i, f: (0, f)),
                  pl.BlockSpec((D_MODEL, TF_FF), lambda i, f: (0, f)),
                  pl.BlockSpec((TF_FF, D_MODEL), lambda i, f: (f, 0)),
                  tok(D_MODEL), _mod_spec(0, TM_FF), vec, vec, _mod_spec(1, TM_FF)],
        out_specs=[tok(D_MODEL), tok(D_MODEL)],
        out_shape=[jax.ShapeDtypeStruct((N_TOK, D_MODEL), F32),
                   jax.ShapeDtypeStruct((N_TOK, D_MODEL), BF16)],
        scratch_shapes=[pltpu.VMEM((TM_FF, D_MODEL), F32)],
        compiler_params=_cparams(("arbitrary", "arbitrary")),
        name="ffn",
    )(h2, w1, w3, w2, x1, mod, ln_g, ln_b, mod)


def _l1_in_kernel(h_ref, wz_ref, wx_ref, wd_ref, z_ref, xbc_ref, dt_ref):
    h = h_ref[...]
    z_ref[...] = jnp.dot(h, wz_ref[...], preferred_element_type=F32).astype(BF16)
    xbc_ref[...] = jnp.dot(h, wx_ref[...], preferred_element_type=F32)
    dt_ref[...] = jnp.dot(h, wd_ref[...], preferred_element_type=F32)


def _l1_in(h3, wz, wx, wd):
    full = lambda shape: pl.BlockSpec(shape, lambda i: (0,) * len(shape))
    tok = lambda w: pl.BlockSpec((TB, w), lambda i: (i, 0))
    return pl.pallas_call(
        _l1_in_kernel,
        grid=(NB,),
        in_specs=[tok(D_MODEL), full((D_MODEL, S_INNER)), full((D_MODEL, S_CONV_CH)), full((D_MODEL, LANES))],
        out_specs=[tok(S_INNER), tok(S_CONV_CH), tok(LANES)],
        out_shape=[jax.ShapeDtypeStruct((N_TOK, S_INNER), BF16),
                   jax.ShapeDtypeStruct((N_TOK, S_CONV_CH), F32),
                   jax.ShapeDtypeStruct((N_TOK, LANES), F32)],
        compiler_params=_cparams(("arbitrary",)),
        name="l1_in",
    )(h3, wz, wx, wd)


def _l1_conv_kernel(prev_ref, xbc_ref, next_ref, cw_ref, cb_ref, dtr_ref, dtb_ref, a_ref,
                    xs_ref, bmt_ref, cm_ref, dt_ref, cum_ref, ext_ref):
    _fill_ext(ext_ref, prev_ref, xbc_ref, next_ref)
    cw = 512
    for j in range(S_INNER // cw):
        c0, c1 = j * cw, (j + 1) * cw
        xs_ref[:, c0:c1] = _silu(_conv_cols(ext_ref, cw_ref, cb_ref, c0, c1)).astype(BF16)
    for j in range(S_GN // cw):
        c0, c1 = S_INNER + j * cw, S_INNER + (j + 1) * cw
        bm = _silu(_conv_cols(ext_ref, cw_ref, cb_ref, c0, c1))
        bmt_ref[j * cw:(j + 1) * cw, :] = bm.T.astype(BF16)
        c0, c1 = c0 + S_GN, c1 + S_GN
        cm_ref[:, j * cw:(j + 1) * cw] = _silu(_conv_cols(ext_ref, cw_ref, cb_ref, c0, c1)).astype(BF16)
    x = dtr_ref[...] + dtb_ref[...]
    dt = jnp.maximum(x, 0.0) + jnp.log1p(jnp.exp(-jnp.abs(x)))
    dt_ref[...] = dt
    da = dt * a_ref[...]
    lower, upper = _tri_masks()
    lane_c = lax.broadcasted_iota(jnp.int32, (CHUNK, LANES), 1)
    for c in range(CPB):
        dc = da[c * CHUNK:(c + 1) * CHUNK, :]
        pref = _tri_dot(lower.astype(BF16), dc)
        suff = _tri_dot(upper.astype(BF16), dc)
        cum_ref[c * CHUNK:(c + 1) * CHUNK, :] = jnp.where(lane_c < S_HEADS, pref, suff)


def _l1_conv(xbc, conv_w, conv_b, dt_raw, dt_bias, a_neg):
    full = lambda shape: pl.BlockSpec(shape, lambda i: (0,) * len(shape))
    tok = lambda w: pl.BlockSpec((TB, w), lambda i: (i, 0))
    return pl.pallas_call(
        _l1_conv_kernel,
        grid=(NB,),
        in_specs=_halo_specs(S_CONV_CH) + [
            full((CONV_W, S_CONV_CH)), full((1, S_CONV_CH)), tok(LANES), full((1, LANES)), full((1, LANES))],
        out_specs=[tok(S_INNER), pl.BlockSpec((S_GN, TB), lambda i: (0, i)), tok(S_GN), tok(LANES), tok(LANES)],
        out_shape=[jax.ShapeDtypeStruct((N_TOK, S_INNER), BF16),
                   jax.ShapeDtypeStruct((S_GN, N_TOK), BF16),
                   jax.ShapeDtypeStruct((N_TOK, S_GN), BF16),
                   jax.ShapeDtypeStruct((N_TOK, LANES), F32),
                   jax.ShapeDtypeStruct((N_TOK, LANES), F32)],
        scratch_shapes=[pltpu.VMEM((TB + 2 * HALO, S_CONV_CH), F32)],
        compiler_params=_cparams(("arbitrary",)),
        name="l1_conv",
    )(xbc, xbc, xbc, conv_w, conv_b, dt_raw, dt_bias, a_neg)


GW = S_HPG * S_HEADDIM


def _ssd_chunk(r, x, bmt, cm, dgt, dsk, st_ref, y_ref, rows):
    lower, upper = _tri_masks()
    mask = lower if r == 0 else upper
    x = x.astype(F32)
    st = st_ref[r]
    cb = jnp.dot(cm, bmt, preferred_element_type=F32)
    lane = lax.broadcasted_iota(jnp.int32, (CHUNK, GW), 1)
    column = lambda row: jnp.broadcast_to(row, (CHUNK, CHUNK)).T
    ws, xd, cum_cols, dt_cols = [], [], [], []
    for j in range(S_HPG):
        c = S_HPG * r + j
        dt_row = dgt[c:c + 1, :]
        cum_row = dgt[N_DIR * S_HPG + c:N_DIR * S_HPG + c + 1, :]
        cum_cols.append(column(cum_row))
        dt_cols.append(column(dt_row))
        w = jnp.exp(jnp.where(mask, cum_cols[j] - cum_row, -jnp.inf)) * cb * dt_row
        ws.append(w.astype(BF16))
        head = jnp.logical_and(lane >= j * S_HEADDIM, lane < (j + 1) * S_HEADDIM)
        xd.append(jnp.where(head, x, 0.0).astype(BF16))
    y = jnp.dot(jnp.concatenate(ws, axis=1), jnp.concatenate(xd, axis=0), preferred_element_type=F32)

    def per_head_lanes(cols):
        lane_c = lax.broadcasted_iota(jnp.int32, (CHUNK, CHUNK), 1)
        per = CHUNK // S_HEADDIM
        parts = []
        for g in range(S_HPG // per):
            part = cols[g * per]
            for k in range(1, per):
                part = jnp.where(lane_c >= k * S_HEADDIM, cols[g * per + k], part)
            parts.append(part)
        return jnp.concatenate(parts, axis=1)

    cum_b = per_head_lanes(cum_cols)
    dt_b = per_head_lanes(dt_cols)
    y = y + jnp.exp(cum_b) * _bdot(cm, st)
    if r == 0:
        y = y + dsk * x
    y_ref[rows, :] = y.astype(BF16)
    last_b = cum_b[CHUNK - 1:CHUNK, :] if r == 0 else cum_b[0:1, :]
    xw = x * (jnp.exp(last_b - cum_b) * dt_b)
    st_ref[r] = jnp.exp(last_b) * st + _bdot(bmt, xw)


def _ssd_kernel(xf_ref, bf_ref, cf_ref, dtf_ref, xb_ref, bb_ref, cb_ref, dtb_ref,
                dsk_ref, s0_ref, yf_ref, yb_ref, so_ref, st_ref):
    s = pl.program_id(1)

    @pl.when(_seq_start(s))
    def _():
        is_ctx = s < NB_CTX
        for r in range(N_DIR):
            s0 = jnp.concatenate([s0_ref[r, j] for j in range(S_HPG)], axis=0)
            st_ref[r] = jnp.where(is_ctx, 0.0, s0.T)

    dsk = dsk_ref[...]
    for c in range(CPB):
        rows = slice(c * CHUNK, (c + 1) * CHUNK)
        _ssd_chunk(0, xf_ref[rows, :], bf_ref[:, rows], cf_ref[rows, :], dtf_ref[:, rows],
                   dsk, st_ref, yf_ref, rows)
    for c in reversed(range(CPB)):
        rows = slice(c * CHUNK, (c + 1) * CHUNK)
        _ssd_chunk(1, xb_ref[rows, :], bb_ref[:, rows], cb_ref[rows, :], dtb_ref[:, rows],
                   dsk, st_ref, yb_ref, rows)

    @pl.when(s < NB_CTX)
    def _():
        for r in range(N_DIR):
            stt = st_ref[r].T
            for j in range(S_HPG):
                so_ref[r, j] = stt[j * S_HEADDIM:(j + 1) * S_HEADDIM, :]


def _ssd_scan(xs, bmt, cm, dgt, dsk, s0):
    ng = 2 * N_DIR * S_HPG
    def side(blk):
        return [
            pl.BlockSpec((TB, GW), lambda g, s: (blk(s), g)),
            pl.BlockSpec((S_STATE, TB), lambda g, s: (g, blk(s))),
            pl.BlockSpec((TB, S_STATE), lambda g, s: (blk(s), g)),
            pl.BlockSpec((None, ng, TB), lambda g, s: (g, 0, blk(s))),
        ]
    ident = lambda s: s
    lat = _lat_seq_of_block
    ctx = lambda s: jnp.minimum(s, NB_CTX - 1)
    st_block = (None, None, N_DIR, S_HPG, S_HEADDIM, S_STATE)
    return pl.pallas_call(
        _ssd_kernel,
        grid=(S_GROUPS, NB),
        in_specs=side(ident) + side(_bwd_block) + [
            pl.BlockSpec((None, 1, GW), lambda g, s: (g, 0, 0)),
            pl.BlockSpec(st_block, lambda g, s: (lat(s), 0, 0, g, 0, 0)),
        ],
        out_specs=[
            pl.BlockSpec((TB, GW), lambda g, s: (s, g)),
            pl.BlockSpec((TB, GW), lambda g, s: (_bwd_block(s), g)),
            pl.BlockSpec(st_block, lambda g, s: (ctx(s), 0, 0, g, 0, 0)),
        ],
        out_shape=[
            jax.ShapeDtypeStruct((N_TOK, S_INNER), BF16),
            jax.ShapeDtypeStruct((N_TOK, S_INNER), BF16),
            jax.ShapeDtypeStruct((BATCH, 1, N_DIR, S_HEADS, S_HEADDIM, S_STATE), F32),
        ],
        scratch_shapes=[pltpu.VMEM((N_DIR, S_STATE, GW), F32)],
        compiler_params=_cparams(("arbitrary", "arbitrary")),
        name="ssd_scan",
    )(xs, bmt, cm, dgt, xs, bmt, cm, dgt, dsk, s0)


def _l1_out_kernel(yf_ref, yb_ref, z_ref, nw_ref, w_ref, x_ref, mod_ref, lg_ref, lb_ref, wr_ref,
                   x3_ref, h4_ref, gates_ref):
    y = (yf_ref[...].astype(F32) + yb_ref[...].astype(F32)) * _silu(z_ref[...].astype(F32))
    y = y * lax.rsqrt(jnp.mean(y * y, axis=-1, keepdims=True) + RMS_EPS) * nw_ref[...]
    out = _bdot(y, w_ref[...])
    x3 = _post_ln(x_ref[...], out, mod_ref[2:3, :], lg_ref[...], lb_ref[...])
    x3_ref[...] = x3
    h4 = x3 * (1.0 + mod_ref[4:5, :]) + mod_ref[3:4, :]
    for cc in range(ROW_TILE):
        h4_ref[pl.ds(cc, TB, stride=ROW_TILE), :] = h4[:, cc * LANES:(cc + 1) * LANES]
    lane = lax.broadcasted_iota(jnp.int32, (TB, LANES), 1)
    logits = jnp.full((TB, LANES), -jnp.inf, F32)
    for e in range(N_EXPERTS):
        le = jnp.sum(h4 * wr_ref[e:e + 1, :], axis=-1, keepdims=True)
        logits = jnp.where(lane == e, le, logits)
    v1 = jnp.max(logits, axis=-1, keepdims=True)
    i1 = jnp.min(jnp.where(logits == v1, lane, LANES), axis=-1, keepdims=True)
    rest = jnp.where(lane == i1, -jnp.inf, logits)
    v2 = jnp.max(rest, axis=-1, keepdims=True)
    i2 = jnp.min(jnp.where(rest == v2, lane, LANES), axis=-1, keepdims=True)
    e2 = jnp.exp(v2 - v1)
    w1 = 1.0 / (1.0 + e2)
    w2 = e2 / (1.0 + e2)
    route = jnp.where(lane == 0, w1, jnp.where(lane == 1, w2, 0.0))
    route = jnp.where(lane == 2, i1.astype(F32), jnp.where(lane == 3, i2.astype(F32), route))
    gates_ref[...] = route


def _l1_out(yf, yb, z, norm_w, w_out, x2, mod, ln_g, ln_b, wr_t):
    full = lambda shape: pl.BlockSpec(shape, lambda i: (0,) * len(shape))
    tok = lambda w: pl.BlockSpec((TB, w), lambda i: (i, 0))
    return pl.pallas_call(
        _l1_out_kernel,
        grid=(NB,),
        in_specs=[tok(S_INNER), tok(S_INNER), tok(S_INNER), full((1, S_INNER)), full((S_INNER, D_MODEL)),
                  tok(D_MODEL), _mod_spec(1, TB), full((1, D_MODEL)), full((1, D_MODEL)),
                  full((N_EXPERTS, D_MODEL))],
        out_specs=[tok(D_MODEL), pl.BlockSpec((TB * ROW_TILE, LANES), lambda i: (i, 0)), tok(LANES)],
        out_shape=[jax.ShapeDtypeStruct((N_TOK, D_MODEL), F32),
                   jax.ShapeDtypeStruct((N_TOK * ROW_TILE, LANES), F32),
                   jax.ShapeDtypeStruct((N_TOK, LANES), F32)],
        compiler_params=_cparams(("arbitrary",)),
        name="l1_out",
    )(yf, yb, z, norm_w, w_out, x2, mod, ln_g, ln_b, wr_t)


TM_E = 512
N_ASSIGN = TOP_K * N_TOK
P_ROWS = N_ASSIGN + N_EXPERTS * TM_E
N_PT = P_ROWS // TM_E
ROW_TILE = D_MODEL // LANES
NF_MOE = E_FF // TF_MOE


def _row(i):
    return pl.ds(pl.multiple_of(i * ROW_TILE, ROW_TILE), ROW_TILE)


def _expert_kernel(te_ref, nv_ref, src_ref, srcn_ref, asg_ref, asgp_ref, h_ref, w1_ref, w3_ref, w2_ref, yc_ref,
                   xbuf, xb_ref, acc_ref, ybuf, gsem, ssem):
    j, f = pl.program_id(0), pl.program_id(1)
    n_valid = nv_ref[0]
    valid = j < n_valid
    first_f, last_f = f == 0, f == NF_MOE - 1

    def gather_row(idx_ref, r):
        pltpu.make_async_copy(h_ref.at[_row(idx_ref[0, r])], xbuf.at[_row(r)], gsem).start()

    def scatter_row(idx_ref, r):
        pltpu.make_async_copy(ybuf.at[_row(r)], yc_ref.at[_row(idx_ref[0, r])], ssem).start()

    def looped(row_fn, idx_ref):
        def body(r, carry):
            row_fn(idx_ref, r)
            return carry
        lax.fori_loop(0, TM_E, body, 0, unroll=8)

    def wait_gather():
        pltpu.make_async_copy(h_ref.at[pl.ds(0, TM_E * ROW_TILE)], xbuf, gsem).wait()

    def wait_scatter():
        pltpu.make_async_copy(ybuf, yc_ref.at[pl.ds(0, TM_E * ROW_TILE)], ssem).wait()

    def load_rows():
        for cc in range(ROW_TILE):
            xb_ref[:, cc * LANES:(cc + 1) * LANES] = xbuf[pl.ds(cc, TM_E, stride=ROW_TILE), :].astype(BF16)
        acc_ref[...] = jnp.zeros_like(acc_ref)

    def swiglu_step():
        x = xb_ref[...]
        a = jnp.dot(x, w1_ref[...], preferred_element_type=F32)
        b = jnp.dot(x, w3_ref[...], preferred_element_type=F32)
        acc_ref[...] += _bdot(_silu(a) * b, w2_ref[...])

    @pl.when(jnp.logical_and(j == 0, first_f))
    def _():
        looped(gather_row, src_ref)

    @pl.when(jnp.logical_and(first_f, j <= n_valid))
    def _():
        wait_gather()

    @pl.when(jnp.logical_and(valid, jnp.logical_and(first_f, j == 0)))
    def _():
        load_rows()
        swiglu_step()
        for r in range(TM_E):
            gather_row(srcn_ref, r)

    @pl.when(jnp.logical_and(valid, jnp.logical_and(first_f, j > 0)))
    def _():
        load_rows()
        swiglu_step()
        for r in range(TM_E):
            gather_row(srcn_ref, r)
        for r in range(TM_E):
            scatter_row(asgp_ref, r)

    @pl.when(jnp.logical_and(valid, last_f))
    def _():
        swiglu_step()

    @pl.when(jnp.logical_and(valid, last_f))
    def _():
        @pl.when(j > 0)
        def _():
            wait_scatter()

        for cc in range(ROW_TILE):
            ybuf[pl.ds(cc, TM_E, stride=ROW_TILE), :] = acc_ref[:, cc * LANES:(cc + 1) * LANES]

    @pl.when(jnp.logical_and(first_f, j == n_valid))
    def _():
        looped(scatter_row, asgp_ref)
        wait_scatter()

    @pl.when(jnp.logical_and(jnp.logical_not(valid), last_f))
    def _():
        ybuf[...] = jnp.zeros_like(ybuf)
        looped(scatter_row, asg_ref)
        wait_scatter()


def _experts(tile_expert, n_valid, src, asg, h4, w1, w3, w2):
    assert NF_MOE >= 2
    fidx = lambda j, f, nv: jnp.where(j < nv[0], f, NF_MOE - 1)
    smem_tile = lambda off: pl.BlockSpec(
        (None, 1, TM_E), lambda j, f, te, nv: (jnp.clip(j + off, 0, N_PT - 1), 0, 0), memory_space=pltpu.SMEM)
    src3 = src.reshape(N_PT, 1, TM_E)
    asg3 = asg.reshape(N_PT, 1, TM_E)
    return pl.pallas_call(
        _expert_kernel,
        grid_spec=pltpu.PrefetchScalarGridSpec(
            num_scalar_prefetch=2,
            grid=(N_PT, NF_MOE),
            in_specs=[smem_tile(0), smem_tile(1), smem_tile(0), smem_tile(-1),
                      pl.BlockSpec(memory_space=pl.ANY),
                      pl.BlockSpec((None, D_MODEL, TF_MOE), lambda j, f, te, nv: (te[j], 0, fidx(j, f, nv))),
                      pl.BlockSpec((None, D_MODEL, TF_MOE), lambda j, f, te, nv: (te[j], 0, fidx(j, f, nv))),
                      pl.BlockSpec((None, TF_MOE, D_MODEL), lambda j, f, te, nv: (te[j], fidx(j, f, nv), 0))],
            out_specs=pl.BlockSpec(memory_space=pl.ANY),
            scratch_shapes=[pltpu.VMEM((TM_E * ROW_TILE, LANES), F32),
                            pltpu.VMEM((TM_E, D_MODEL), BF16),
                            pltpu.VMEM((TM_E, D_MODEL), F32),
                            pltpu.VMEM((TM_E * ROW_TILE, LANES), F32),
                            pltpu.SemaphoreType.DMA(()),
                            pltpu.SemaphoreType.DMA(())],
        ),
        out_shape=jax.ShapeDtypeStruct((P_ROWS * ROW_TILE, LANES), F32),
        compiler_params=_cparams(("arbitrary", "arbitrary")),
        name="experts",
    )(tile_expert, n_valid, src3, src3, asg3, asg3, h4, w1, w3, w2)


def _moe_out_kernel(yc_ref, r_ref, x_ref, mod_ref, lg_ref, lb_ref, yp_ref, ys_ref, f_ref):
    i = pl.program_id(0)
    w1, w2 = r_ref[:, 0:1], r_ref[:, 1:2]
    for cc in range(ROW_TILE):
        y_top1 = yc_ref[pl.ds(cc, TM_FF, stride=TOP_K * ROW_TILE), :]
        y_top2 = yc_ref[pl.ds(ROW_TILE + cc, TM_FF, stride=TOP_K * ROW_TILE), :]
        f_ref[:, cc * LANES:(cc + 1) * LANES] = w1 * y_top1 + w2 * y_top2
    y = _post_ln(x_ref[...], f_ref[...], mod_ref[5:6, :], lg_ref[...], lb_ref[...])

    @pl.when(i < N_CTX // TM_FF)
    def _():
        yp_ref[...] = y

    @pl.when(i >= N_CTX // TM_FF)
    def _():
        ys_ref[...] = y


def _moe_out(yc, route, x3, mod, ln_g, ln_b):
    nctx = N_CTX // TM_FF
    tok = lambda w: pl.BlockSpec((TM_FF, w), lambda i: (i, 0))
    vec = pl.BlockSpec((1, D_MODEL), lambda i: (0, 0))
    return pl.pallas_call(
        _moe_out_kernel,
        grid=(N_TOK // TM_FF,),
        in_specs=[pl.BlockSpec((TM_FF * TOP_K * ROW_TILE, LANES), lambda i: (i, 0)),
                  tok(LANES), tok(D_MODEL), _mod_spec(1, TM_FF), vec, vec],
        out_specs=[pl.BlockSpec((TM_FF, D_MODEL), lambda i: (jnp.minimum(i, nctx - 1), 0)),
                   pl.BlockSpec((TM_FF, D_MODEL), lambda i: (jnp.maximum(i - nctx, 0), 0))],
        out_shape=[jax.ShapeDtypeStruct((N_CTX, D_MODEL), F32),
                   jax.ShapeDtypeStruct((N_LAT, D_MODEL), F32)],
        scratch_shapes=[pltpu.VMEM((TM_FF, D_MODEL), F32)],
        compiler_params=_cparams(("arbitrary",)),
        name="moe_out",
    )(yc, route, x3, mod, ln_g, ln_b)


def _routing_tables(route):
    flat_e = route[:, 2:2 + TOP_K].astype(jnp.int32).reshape(N_ASSIGN)
    onehot = (flat_e[:, None] == jnp.arange(N_EXPERTS, dtype=jnp.int32)[None, :]).astype(jnp.int32)
    csum = jnp.cumsum(onehot, axis=0)
    rank = jnp.take_along_axis(csum, flat_e[:, None], axis=1)[:, 0] - 1
    ntile = (csum[-1] + TM_E - 1) // TM_E
    tile_end = jnp.cumsum(ntile)
    dest = (tile_end - ntile)[flat_e] * TM_E + rank
    asg = jnp.full((P_ROWS,), -1, jnp.int32).at[dest].set(jnp.arange(N_ASSIGN, dtype=jnp.int32))
    is_pad = asg < 0
    src = jnp.maximum(asg, 0) // TOP_K
    asg = jnp.where(is_pad, N_ASSIGN - 1 + jnp.cumsum(is_pad.astype(jnp.int32)), asg)
    tiles = jnp.arange(N_PT, dtype=jnp.int32)
    n_valid = tile_end[-1]
    tile_expert = jnp.sum((jnp.minimum(tiles, n_valid - 1)[:, None] >= tile_end[None, :]).astype(jnp.int32), axis=1)
    return src, asg, tile_expert, n_valid.reshape(1)


def _grid_pos_embed():
    rows = DEC_SEQ // GRID_W
    quarter = D_MODEL // 4
    freqs = jnp.exp(-math.log(POS_BASE) * jnp.arange(quarter, dtype=F32) / quarter)
    er = jnp.arange(rows, dtype=F32)[:, None] * freqs
    ec = jnp.arange(GRID_W, dtype=F32)[:, None] * freqs
    row_emb = jnp.concatenate([jnp.sin(er), jnp.cos(er)], axis=-1)
    col_emb = jnp.concatenate([jnp.sin(ec), jnp.cos(ec)], axis=-1)
    pos = jnp.concatenate([jnp.broadcast_to(row_emb[:, None, :], (rows, GRID_W, D_MODEL // 2)),
                           jnp.broadcast_to(col_emb[None, :, :], (rows, GRID_W, D_MODEL // 2))], axis=-1)
    return pos.reshape(DEC_SEQ, D_MODEL)


def _pad_lanes(a, width=LANES):
    return jnp.pad(a, [(0, 0)] * (a.ndim - 1) + [(0, width - a.shape[-1])])


def kernel(x_prompt, x_sample, c, state_mlstm_C, state_mlstm_n, state_mlstm_m, state_ssm, c_ctx,
           ada_w, ada_b, ln_g, ln_b,
           ml_w_up, ml_conv_w, ml_conv_b, ml_wq, ml_wk, ml_wv, ml_w_ig, ml_b_ig, ml_w_fg, ml_b_fg,
           ml_norm_w, ml_skip, ml_w_down,
           ss_w_in, ss_conv_w, ss_conv_b, ss_dt_bias, ss_a_log, ss_d, ss_norm_w, ss_w_out,
           ff_w1, ff_w3, ff_w2,
           moe_w_router, moe_w1, moe_w3, moe_w2):
    row = lambda a: a.reshape(1, -1)
    cond = jnp.concatenate([c_ctx[None, :], c, jnp.zeros((COND_ROWS - N_COND, D_MODEL), F32)], axis=0)
    mod = _ada_mod(cond, ada_w.astype(BF16), ada_b)

    xa, xm, z0 = _l0_in(x_prompt.reshape(N_CTX, D_MODEL), x_sample.reshape(N_LAT, D_MODEL), _grid_pos_embed(),
                        mod, ml_w_up[0].astype(BF16))
    ng = N_DIR * M_HEADS
    wg = jnp.concatenate([ml_w_ig[0].transpose(1, 0, 2).reshape(3 * M_INNER, ng),
                          ml_w_fg[0].transpose(1, 0, 2).reshape(3 * M_INNER, ng)], axis=1)
    bg = jnp.concatenate([ml_b_ig[0].reshape(ng), ml_b_fg[0].reshape(ng)])
    xc, q, k, v, g, bc = _l0_qkv(xm, ml_conv_w[0], row(ml_conv_b[0]),
                                 ml_wq[0].astype(BF16), ml_wk[0].astype(BF16), ml_wv[0].astype(BF16),
                                 _pad_lanes(wg).astype(BF16), row(_pad_lanes(bg)))
    gh = jnp.concatenate([g[:, :ng].reshape(N_TOK, N_DIR, M_HEADS),
                          bc[:, ng:2 * ng].reshape(N_TOK, N_DIR, M_HEADS)], axis=1)
    ght = gh.transpose(2, 1, 0)
    gh = gh.transpose(2, 0, 1)
    hf, hb, new_c, new_n, new_m = _mlstm_scan(
        q, k, v, gh, ght, state_mlstm_C,
        state_mlstm_n.reshape(DEC_BATCH, N_DIR, M_HEADS, 1, M_DH),
        state_mlstm_m.reshape(DEC_BATCH, N_DIR, M_HEADS, 1, 1))
    x1, h2 = _l0_out(hf, hb, xc, z0, row(ml_norm_w[0]), row(ml_skip[0]), ml_w_down[0].astype(BF16),
                     xa, mod, row(ln_g[0, 0]), row(ln_b[0, 0]))
    x2, h3 = _ffn(h2, ff_w1[0].astype(BF16), ff_w3[0].astype(BF16), ff_w2[0].astype(BF16),
                  x1, mod, row(ln_g[0, 1]), row(ln_b[0, 1]))

    w_in = ss_w_in[0]
    z1, xbc, dt_raw = _l1_in(h3, w_in[:, :S_INNER].astype(BF16),
                             w_in[:, S_INNER:S_INNER + S_CONV_CH].astype(BF16),
                             _pad_lanes(w_in[:, S_INNER + S_CONV_CH:]).astype(BF16))
    a_neg = -jnp.exp(ss_a_log[0].reshape(N_DIR * S_HEADS))
    xs, bmt, cm, dt, cum = _l1_conv(xbc, ss_conv_w[0], row(ss_conv_b[0]), dt_raw,
                                    row(_pad_lanes(ss_dt_bias[0].reshape(N_DIR * S_HEADS))), row(_pad_lanes(a_neg)))
    nh = N_DIR * S_HEADS
    to_group = lambda a: a[:, :nh].reshape(N_TOK, N_DIR, S_GROUPS, S_HPG).transpose(2, 0, 1, 3).reshape(
        S_GROUPS, N_TOK, N_DIR * S_HPG)
    dgt = jnp.concatenate([to_group(dt), to_group(cum)], axis=-1).transpose(0, 2, 1)
    dsk = jnp.repeat(ss_d[0], S_HEADDIM).reshape(S_GROUPS, 1, GW)
    yf, yb, new_s = _ssd_scan(xs, bmt, cm, dgt, dsk, state_ssm)
    x3, h4, route = _l1_out(yf, yb, z1, row(ss_norm_w[0]), ss_w_out[0].astype(BF16), x2, mod,
                            row(ln_g[1, 0]), row(ln_b[1, 0]), moe_w_router[0].T)
    src, asg, tile_expert, n_valid = _routing_tables(route)
    yc = _experts(tile_expert, n_valid, src, asg, h4,
                  moe_w1[0].astype(BF16), moe_w3[0].astype(BF16), moe_w2[0].astype(BF16))
    yp, ys = _moe_out(yc, route, x3, mod, row(ln_g[1, 1]), row(ln_b[1, 1]))

    return (yp.reshape(BATCH, SEQ, D_MODEL), ys.reshape(DEC_BATCH, DEC_SEQ, D_MODEL),
            new_c,
            new_n.reshape(BATCH, 1, N_DIR, M_HEADS, M_DH),
            new_m.reshape(BATCH, 1, N_DIR, M_HEADS),
            new_s)
```

```python
import functools
import math

import jax
import jax.numpy as jnp
from jax import lax
from jax.experimental import pallas as pl
from jax.experimental.pallas import tpu as pltpu

F32 = jnp.float32
BF16 = jnp.bfloat16

D_MODEL = 1024
BATCH = 16
SEQ = 256
DEC_BATCH = 4
DEC_SEQ = 4096
GRID_W = 64
CHUNK = 128
CONV_W = 5
N_DIR = 2
M_INNER = 2 * D_MODEL
M_HEADS = 4
M_DH = M_INNER // M_HEADS
S_INNER = 2 * D_MODEL
S_HEADDIM = 64
S_HEADS = S_INNER // S_HEADDIM
S_GROUPS = 8
S_HPG = S_HEADS // S_GROUPS
S_STATE = 128
S_GN = S_GROUPS * S_STATE
S_CONV_CH = S_INNER + 2 * S_GN
D_FF = 2816
N_EXPERTS = 8
TOP_K = 2
E_FF = 3584
DEPTH = 2
ALPHA = (2 * DEPTH) ** 0.25
LN_EPS = 1e-5
RMS_EPS = 1e-5
POS_BASE = 10000.0

N_CTX = BATCH * SEQ
N_LAT = DEC_BATCH * DEC_SEQ
N_TOK = N_CTX + N_LAT
N_COND = 1 + DEC_BATCH
COND_ROWS = 8

TB = 256
NB = N_TOK // TB
NB_CTX = N_CTX // TB
NB_SEQ = DEC_SEQ // TB
HALO = 8
CPB = TB // CHUNK
MCH = TB
C_SLAB = 256

TM_FF = 512
TF_FF = 1408
TF_MOE = 1792
LANES = 128

VMEM_LIMIT = 56 * 1024 * 1024


def _cparams(sem):
    return pltpu.CompilerParams(dimension_semantics=sem, vmem_limit_bytes=VMEM_LIMIT)


def _cond_of_token(t0):
    return jnp.where(t0 < N_CTX, 0, 1 + (t0 - N_CTX) // DEC_SEQ)


def _lat_seq_of_block(s):
    return jnp.maximum(s - NB_CTX, 0) // NB_SEQ


def _bwd_block(s):
    k = s - NB_CTX
    return jnp.where(s < NB_CTX, s, NB_CTX + (k // NB_SEQ) * NB_SEQ + (NB_SEQ - 1 - k % NB_SEQ))


def _seq_start(s):
    return jnp.logical_or(s < NB_CTX, (s - NB_CTX) % NB_SEQ == 0)


def _seq_end(s):
    return jnp.logical_or(s < NB_CTX, (s - NB_CTX) % NB_SEQ == NB_SEQ - 1)


def _silu(x):
    return x * jax.nn.sigmoid(x)


def _bdot(a, b):
    return jnp.dot(a.astype(BF16), b.astype(BF16), preferred_element_type=F32)


def _tri_dot(tri, x):
    hi = x.astype(BF16)
    r1 = x - hi.astype(F32)
    mid = r1.astype(BF16)
    lo = (r1 - mid.astype(F32)).astype(BF16)
    d = lambda a: jnp.dot(tri, a, preferred_element_type=F32)
    return d(hi) + d(mid) + d(lo)


def _tri_masks(n=CHUNK):
    t = lax.broadcasted_iota(jnp.int32, (n, n), 0)
    u = lax.broadcasted_iota(jnp.int32, (n, n), 1)
    return u <= t, u >= t


def _layer_norm(r, g, b):
    mu = jnp.mean(r, axis=-1, keepdims=True)
    d = r - mu
    var = jnp.mean(d * d, axis=-1, keepdims=True)
    return d * lax.rsqrt(var + LN_EPS) * g + b


def _ada_kernel(cond_ref, w_ref, b_ref, o_ref):
    o_ref[...] = _bdot(_silu(cond_ref[...]), w_ref[...]) + b_ref[...]


def _ada_mod(cond, ada_w, ada_b):
    ncol = 6 * D_MODEL // D_MODEL
    out = pl.pallas_call(
        _ada_kernel,
        grid=(DEPTH, ncol),
        in_specs=[
            pl.BlockSpec((COND_ROWS, D_MODEL), lambda l, j: (0, 0)),
            pl.BlockSpec((None, D_MODEL, D_MODEL), lambda l, j: (l, 0, j)),
            pl.BlockSpec((None, 1, D_MODEL), lambda l, j: (l, 0, j)),
        ],
        out_specs=pl.BlockSpec((None, COND_ROWS, D_MODEL), lambda l, j: (l, 0, j)),
        out_shape=jax.ShapeDtypeStruct((DEPTH, COND_ROWS, 6 * D_MODEL), F32),
        compiler_params=_cparams(("arbitrary", "arbitrary")),
        name="ada_mod",
    )(cond, ada_w, ada_b.reshape(DEPTH, 1, 6 * D_MODEL))
    return out.reshape(DEPTH, COND_ROWS, 6, D_MODEL)


def _mod_spec(layer, tm):
    return pl.BlockSpec((None, None, 6, D_MODEL), lambda i, *_: (layer, _cond_of_token(i * tm), 0, 0))


def _l0_in_kernel(xp_ref, xs_ref, pos_ref, mod_ref, w_ref, xa_ref, xm_ref, z_ref):
    i = pl.program_id(0)
    x = jnp.where(i >= NB_CTX, xs_ref[...] + pos_ref[...], xp_ref[...])
    xa_ref[...] = x
    h = x * (1.0 + mod_ref[1:2, :]) + mod_ref[0:1, :]
    u = _bdot(h, w_ref[...])
    xm_ref[...] = u[:, :M_INNER]
    z_ref[...] = u[:, M_INNER:].astype(BF16)


def _l0_in(xp, xs, pos, mod, w_up):
    return pl.pallas_call(
        _l0_in_kernel,
        grid=(NB,),
        in_specs=[
            pl.BlockSpec((TB, D_MODEL), lambda i: (jnp.minimum(i, NB_CTX - 1), 0)),
            pl.BlockSpec((TB, D_MODEL), lambda i: (jnp.maximum(i - NB_CTX, 0), 0)),
            pl.BlockSpec((TB, D_MODEL), lambda i: (jnp.maximum(i - NB_CTX, 0) % NB_SEQ, 0)),
            _mod_spec(0, TB),
            pl.BlockSpec((D_MODEL, 2 * M_INNER), lambda i: (0, 0)),
        ],
        out_specs=[
            pl.BlockSpec((TB, D_MODEL), lambda i: (i, 0)),
            pl.BlockSpec((TB, M_INNER), lambda i: (i, 0)),
            pl.BlockSpec((TB, M_INNER), lambda i: (i, 0)),
        ],
        out_shape=[
            jax.ShapeDtypeStruct((N_TOK, D_MODEL), F32),
            jax.ShapeDtypeStruct((N_TOK, M_INNER), F32),
            jax.ShapeDtypeStruct((N_TOK, M_INNER), BF16),
        ],
        compiler_params=_cparams(("arbitrary",)),
        name="l0_in",
    )(xp, xs, pos, mod, w_up)


def _fill_ext(ext_ref, prev_ref, main_ref, next_ref):
    i = pl.program_id(0)
    ext_ref[0:HALO, :] = jnp.where(_seq_start(i), 0.0, prev_ref[...])
    ext_ref[HALO:HALO + TB, :] = main_ref[...]
    ext_ref[HALO + TB:HALO + TB + HALO, :] = jnp.where(_seq_end(i), 0.0, next_ref[...])


def _conv_cols(ext_ref, w_ref, b_ref, c0, c1):
    acc = b_ref[:, c0:c1]
    for j in range(CONV_W):
        r0 = HALO - CONV_W // 2 + j
        acc = acc + w_ref[j:j + 1, c0:c1] * ext_ref[r0:r0 + TB, c0:c1]
    return acc


def _halo_specs(width):
    nh = TB // HALO
    last = N_TOK // HALO - 1
    return [
        pl.BlockSpec((HALO, width), lambda i: (jnp.maximum(i * nh - 1, 0), 0)),
        pl.BlockSpec((TB, width), lambda i: (i, 0)),
        pl.BlockSpec((HALO, width), lambda i: (jnp.minimum((i + 1) * nh, last), 0)),
    ]


def _l0_qkv_kernel(prev_ref, xm_ref, next_ref, cw_ref, cb_ref, wq_ref, wk_ref, wv_ref, wg_ref, bg_ref,
                   xc_ref, q_ref, k_ref, v_ref, g_ref, bc_ref, ext_ref):
    _fill_ext(ext_ref, prev_ref, xm_ref, next_ref)
    pre = jnp.zeros((TB, LANES), F32) + bg_ref[...]
    for h in range(M_HEADS):
        c0, c1 = h * M_DH, (h + 1) * M_DH
        xc = _silu(_conv_cols(ext_ref, cw_ref, cb_ref, c0, c1))
        xcb = xc.astype(BF16)
        xc_ref[:, c0:c1] = xcb
        q = jnp.dot(xcb, wq_ref[h], preferred_element_type=F32).astype(BF16)
        k = (jnp.dot(xcb, wk_ref[h], preferred_element_type=F32) * (M_DH ** -0.5)).astype(BF16)
        v = jnp.dot(xm_ref[:, c0:c1].astype(BF16), wv_ref[h], preferred_element_type=F32).astype(BF16)
        q_ref[:, c0:c1] = q
        k_ref[:, c0:c1] = k
        v_ref[:, c0:c1] = v
        pre = pre + jnp.dot(q, wg_ref[c0:c1, :], preferred_element_type=F32)
        pre = pre + jnp.dot(k, wg_ref[M_INNER + c0:M_INNER + c1, :], preferred_element_type=F32)
        pre = pre + jnp.dot(v, wg_ref[2 * M_INNER + c0:2 * M_INNER + c1, :], preferred_element_type=F32)
    lane = lax.broadcasted_iota(jnp.int32, (TB, LANES), 1)
    ng = N_DIR * M_HEADS
    logsig = -(jnp.maximum(-pre, 0.0) + jnp.log1p(jnp.exp(-jnp.abs(pre))))
    g = jnp.where(jnp.logical_and(lane >= ng, lane < 2 * ng), logsig, pre)
    g_ref[...] = g
    lower, upper = _tri_masks(MCH)
    lane_c = lax.broadcasted_iota(jnp.int32, (MCH, LANES), 1)
    fwd_lane = jnp.logical_and(lane_c >= ng, lane_c < ng + M_HEADS)
    for c in range(TB // MCH):
        gc = g[c * MCH:(c + 1) * MCH, :]
        pref = _tri_dot(lower.astype(BF16), gc)
        suff = _tri_dot(upper.astype(BF16), gc)
        bc_ref[c * MCH:(c + 1) * MCH, :] = jnp.where(fwd_lane, pref, suff)


def _l0_qkv(xm, conv_w, conv_b, wq, wk, wv, wg, bg):
    full = lambda shape: pl.BlockSpec(shape, lambda i: (0,) * len(shape))
    tok = lambda w: pl.BlockSpec((TB, w), lambda i: (i, 0))
    return pl.pallas_call(
        _l0_qkv_kernel,
        grid=(NB,),
        in_specs=_halo_specs(M_INNER) + [
            full((CONV_W, M_INNER)), full((1, M_INNER)),
            full((M_HEADS, M_DH, M_DH)), full((M_HEADS, M_DH, M_DH)), full((M_HEADS, M_DH, M_DH)),
            full((3 * M_INNER, LANES)), full((1, LANES)),
        ],
        out_specs=[tok(M_INNER), tok(M_INNER), tok(M_INNER), tok(M_INNER), tok(LANES), tok(LANES)],
        out_shape=[
            jax.ShapeDtypeStruct((N_TOK, M_INNER), BF16),
            jax.ShapeDtypeStruct((N_TOK, M_INNER), BF16),
            jax.ShapeDtypeStruct((N_TOK, M_INNER), BF16),
            jax.ShapeDtypeStruct((N_TOK, M_INNER), BF16),
            jax.ShapeDtypeStruct((N_TOK, LANES), F32),
            jax.ShapeDtypeStruct((N_TOK, LANES), F32),
        ],
        scratch_shapes=[pltpu.VMEM((TB + 2 * HALO, M_INNER), F32)],
        compiler_params=_cparams(("arbitrary",)),
        name="l0_qkv",
    )(xm, xm, xm, conv_w, conv_b, wq, wk, wv, wg, bg)


def _mlstm_chunk(r, q, k, v, gh, ght, c_ref, n_ref, m_ref):
    n = q.shape[0]
    lower, upper = _tri_masks(n)
    mask = lower if r == 0 else upper
    li_col, li_row = gh[:, r:r + 1], ght[r:r + 1, :]
    b_col, b_row = gh[:, 2 + r:3 + r], ght[2 + r:3 + r, :]
    m_prev = m_ref[r]
    n_prev = n_ref[r]
    d = jnp.where(mask, b_col - b_row + li_row, -jnp.inf)
    g = b_col + m_prev
    m_t = jnp.maximum(g, jnp.max(d, axis=1, keepdims=True))
    qk = lax.dot_general(q, k, (((1,), (1,)), ((), ())), preferred_element_type=F32)
    s = qk * jnp.exp(d - m_t)
    inter = jnp.exp(g - m_t)
    den = jnp.sum(s, axis=1, keepdims=True) + inter * jnp.sum(q.astype(F32) * n_prev, axis=1, keepdims=True)
    den = jnp.maximum(jnp.abs(den), jnp.exp(-m_t))
    s = s.astype(BF16)
    b_last = b_col[n - 1:n, :] if r == 0 else b_col[0:1, :]
    ld = b_last - b_col + li_col
    m_new = jnp.maximum(b_last + m_prev, jnp.max(ld, axis=0, keepdims=True))
    wk = jnp.exp(ld - m_new)
    decay = jnp.exp(b_last + m_prev - m_new)
    kw = k.astype(F32) * wk
    n_ref[r] = decay * n_prev + jnp.sum(kw, axis=0, keepdims=True)
    m_ref[r] = m_new
    kwt = kw.T.astype(BF16)
    hs = []
    for e in range(M_DH // C_SLAB):
        cols = slice(e * C_SLAB, (e + 1) * C_SLAB)
        c_slab = c_ref[r, :, cols]
        num = jnp.dot(s, v[:, cols], preferred_element_type=F32) + inter * _bdot(q, c_slab)
        hs.append((num / den).astype(BF16))
        c_ref[r, :, cols] = decay * c_slab + jnp.dot(kwt, v[:, cols], preferred_element_type=F32)
    return jnp.concatenate(hs, axis=1)


def _mlstm_kernel(qf_ref, kf_ref, vf_ref, ghf_ref, ghtf_ref, qb_ref, kb_ref, vb_ref, ghb_ref, ghtb_ref,
                  c0_ref, n0_ref, m0_ref,
                  hf_ref, hb_ref, co_ref, no_ref, mo_ref,
                  c_ref, n_ref, m_ref):
    s = pl.program_id(1)

    @pl.when(_seq_start(s))
    def _():
        is_ctx = s < NB_CTX
        c_ref[...] = jnp.where(is_ctx, 0.0, c0_ref[...])
        n_ref[...] = jnp.where(is_ctx, 0.0, n0_ref[...])
        m_ref[...] = jnp.where(is_ctx, 0.0, m0_ref[...])

    for c in range(TB // MCH):
        rows = slice(c * MCH, (c + 1) * MCH)
        hf_ref[rows, :] = _mlstm_chunk(0, qf_ref[rows, :], kf_ref[rows, :], vf_ref[rows, :],
                                       ghf_ref[rows, :], ghtf_ref[:, rows], c_ref, n_ref, m_ref)
    for c in reversed(range(TB // MCH)):
        rows = slice(c * MCH, (c + 1) * MCH)
        hb_ref[rows, :] = _mlstm_chunk(1, qb_ref[rows, :], kb_ref[rows, :], vb_ref[rows, :],
                                       ghb_ref[rows, :], ghtb_ref[:, rows], c_ref, n_ref, m_ref)

    @pl.when(s < NB_CTX)
    def _():
        co_ref[...] = c_ref[...]
        no_ref[...] = n_ref[...]
        mo_ref[...] = m_ref[...]


def _mlstm_scan(q, k, v, gh, ght, c0, n0, m0):
    fwd = lambda h, s: (s, h)
    bwd = lambda h, s: (_bwd_block(s), h)
    tokf = pl.BlockSpec((TB, M_DH), fwd)
    tokb = pl.BlockSpec((TB, M_DH), bwd)
    ghf = pl.BlockSpec((None, TB, 4), lambda h, s: (h, s, 0))
    ghb = pl.BlockSpec((None, TB, 4), lambda h, s: (h, _bwd_block(s), 0))
    ghtf = pl.BlockSpec((None, 4, TB), lambda h, s: (h, 0, s))
    ghtb = pl.BlockSpec((None, 4, TB), lambda h, s: (h, 0, _bwd_block(s)))
    lat = _lat_seq_of_block
    ctx = lambda s: jnp.minimum(s, NB_CTX - 1)
    return pl.pallas_call(
        _mlstm_kernel,
        grid=(M_HEADS, NB),
        in_specs=[
            tokf, tokf, tokf, ghf, ghtf, tokb, tokb, tokb, ghb, ghtb,
            pl.BlockSpec((None, None, N_DIR, None, M_DH, M_DH), lambda h, s: (lat(s), 0, 0, h, 0, 0)),
            pl.BlockSpec((None, N_DIR, None, 1, M_DH), lambda h, s: (lat(s), 0, h, 0, 0)),
            pl.BlockSpec((None, N_DIR, None, 1, 1), lambda h, s: (lat(s), 0, h, 0, 0)),
        ],
        out_specs=[
            tokf, tokb,
            pl.BlockSpec((None, None, N_DIR, None, M_DH, M_DH), lambda h, s: (ctx(s), 0, 0, h, 0, 0)),
            pl.BlockSpec((None, N_DIR, None, 1, M_DH), lambda h, s: (ctx(s), 0, h, 0, 0)),
            pl.BlockSpec((None, N_DIR, None, 1, 1), lambda h, s: (ctx(s), 0, h, 0, 0)),
        ],
        out_shape=[
            jax.ShapeDtypeStruct((N_TOK, M_INNER), BF16),
            jax.ShapeDtypeStruct((N_TOK, M_INNER), BF16),
            jax.ShapeDtypeStruct((BATCH, 1, N_DIR, M_HEADS, M_DH, M_DH), F32),
            jax.ShapeDtypeStruct((BATCH, N_DIR, M_HEADS, 1, M_DH), F32),
            jax.ShapeDtypeStruct((BATCH, N_DIR, M_HEADS, 1, 1), F32),
        ],
        scratch_shapes=[
            pltpu.VMEM((N_DIR, M_DH, M_DH), F32),
            pltpu.VMEM((N_DIR, 1, M_DH), F32),
            pltpu.VMEM((N_DIR, 1, 1), F32),
        ],
        compiler_params=_cparams(("arbitrary", "arbitrary")),
        name="mlstm_scan",
    )(q, k, v, gh, ght, q, k, v, gh, ght, c0, n0, m0)


def _post_ln(x, branch, gate, ln_g, ln_b):
    return _layer_norm(ALPHA * x + gate * branch, ln_g, ln_b)


def _l0_out_kernel(hf_ref, hb_ref, xc_ref, z_ref, nw_ref, sk_ref, w_ref, xa_ref, mod_ref, lg_ref, lb_ref,
                   x1_ref, h2_ref, o_scr):
    for h in range(M_HEADS):
        c0, c1 = h * M_DH, (h + 1) * M_DH
        hs = hf_ref[:, c0:c1].astype(F32) + hb_ref[:, c0:c1].astype(F32)
        mu = jnp.mean(hs, axis=-1, keepdims=True)
        d = hs - mu
        var = jnp.mean(d * d, axis=-1, keepdims=True)
        hn = d * lax.rsqrt(var + LN_EPS) * nw_ref[:, c0:c1]
        o = (hn + sk_ref[:, c0:c1] * xc_ref[:, c0:c1].astype(F32)) * _silu(z_ref[:, c0:c1].astype(F32))
        o_scr[:, c0:c1] = o.astype(BF16)
    out = jnp.dot(o_scr[...], w_ref[...], preferred_element_type=F32)
    x1 = _post_ln(xa_ref[...], out, mod_ref[2:3, :], lg_ref[...], lb_ref[...])
    x1_ref[...] = x1
    h2_ref[...] = (x1 * (1.0 + mod_ref[4:5, :]) + mod_ref[3:4, :]).astype(BF16)


def _l0_out(hf, hb, xc, z, norm_w, skip, w_down, xa, mod, ln_g, ln_b):
    full = lambda shape: pl.BlockSpec(shape, lambda i: (0,) * len(shape))
    tok = lambda w: pl.BlockSpec((TB, w), lambda i: (i, 0))
    return pl.pallas_call(
        _l0_out_kernel,
        grid=(NB,),
        in_specs=[tok(M_INNER), tok(M_INNER), tok(M_INNER), tok(M_INNER),
                  full((1, M_INNER)), full((1, M_INNER)), full((M_INNER, D_MODEL)),
                  tok(D_MODEL), _mod_spec(0, TB), full((1, D_MODEL)), full((1, D_MODEL))],
        out_specs=[tok(D_MODEL), tok(D_MODEL)],
        out_shape=[jax.ShapeDtypeStruct((N_TOK, D_MODEL), F32),
                   jax.ShapeDtypeStruct((N_TOK, D_MODEL), BF16)],
        scratch_shapes=[pltpu.VMEM((TB, M_INNER), BF16)],
        compiler_params=_cparams(("arbitrary",)),
        name="l0_out",
    )(hf, hb, xc, z, norm_w, skip, w_down, xa, mod, ln_g, ln_b)


def _ffn_kernel(h_ref, w1_ref, w3_ref, w2_ref, x_ref, mod_ref, lg_ref, lb_ref, modn_ref,
                x2_ref, h3_ref, acc_ref):
    f = pl.program_id(1)

    @pl.when(f == 0)
    def _():
        acc_ref[...] = jnp.zeros_like(acc_ref)

    h = h_ref[...]
    a = jnp.dot(h, w1_ref[...], preferred_element_type=F32)
    b = jnp.dot(h, w3_ref[...], preferred_element_type=F32)
    acc_ref[...] += _bdot(_silu(a) * b, w2_ref[...])

    @pl.when(f == pl.num_programs(1) - 1)
    def _():
        x2 = _post_ln(x_ref[...], acc_ref[...], mod_ref[5:6, :], lg_ref[...], lb_ref[...])
        x2_ref[...] = x2
        h3_ref[...] = (x2 * (1.0 + modn_ref[1:2, :]) + modn_ref[0:1, :]).astype(BF16)


def _ffn(h2, w1, w3, w2, x1, mod, ln_g, ln_b):
    tok = lambda w: pl.BlockSpec((TM_FF, w), lambda i, f: (i, 0))
    vec = pl.BlockSpec((1, D_MODEL), lambda i, f: (0, 0))
    return pl.pallas_call(
        _ffn_kernel,
        grid=(N_TOK // TM_FF, D_FF // TF_FF),
        in_specs=[tok(D_MODEL),
                  pl.BlockSpec((D_MODEL, TF_FF), lambda i, f: (0, f)),
                  pl.BlockSpec((D_MODEL, TF_FF), lambda i, f: (0, f)),
                  pl.BlockSpec((TF_FF, D_MODEL), lambda i, f: (f, 0)),
                  tok(D_MODEL), _mod_spec(0, TM_FF), vec, vec, _mod_spec(1, TM_FF)],
        out_specs=[tok(D_MODEL), tok(D_MODEL)],
        out_shape=[jax.ShapeDtypeStruct((N_TOK, D_MODEL), F32),
                   jax.ShapeDtypeStruct((N_TOK, D_MODEL), BF16)],
        scratch_shapes=[pltpu.VMEM((TM_FF, D_MODEL), F32)],
        compiler_params=_cparams(("arbitrary", "arbitrary")),
        name="ffn",
    )(h2, w1, w3, w2, x1, mod, ln_g, ln_b, mod)


def _l1_in_kernel(h_ref, wz_ref, wx_ref, wd_ref, z_ref, xbc_ref, dt_ref):
    h = h_ref[...]
    z_ref[...] = jnp.dot(h, wz_ref[...], preferred_element_type=F32).astype(BF16)
    xbc_ref[...] = jnp.dot(h, wx_ref[...], preferred_element_type=F32)
    dt_ref[...] = jnp.dot(h, wd_ref[...], preferred_element_type=F32)


def _l1_in(h3, wz, wx, wd):
    full = lambda shape: pl.BlockSpec(shape, lambda i: (0,) * len(shape))
    tok = lambda w: pl.BlockSpec((TB, w), lambda i: (i, 0))
    return pl.pallas_call(
        _l1_in_kernel,
        grid=(NB,),
        in_specs=[tok(D_MODEL), full((D_MODEL, S_INNER)), full((D_MODEL, S_CONV_CH)), full((D_MODEL, LANES))],
        out_specs=[tok(S_INNER), tok(S_CONV_CH), tok(LANES)],
        out_shape=[jax.ShapeDtypeStruct((N_TOK, S_INNER), BF16),
                   jax.ShapeDtypeStruct((N_TOK, S_CONV_CH), F32),
                   jax.ShapeDtypeStruct((N_TOK, LANES), F32)],
        compiler_params=_cparams(("arbitrary",)),
        name="l1_in",
    )(h3, wz, wx, wd)


def _l1_conv_kernel(prev_ref, xbc_ref, next_ref, cw_ref, cb_ref, dtr_ref, dtb_ref, a_ref,
                    xs_ref, bmt_ref, cm_ref, dt_ref, cum_ref, ext_ref):
    _fill_ext(ext_ref, prev_ref, xbc_ref, next_ref)
    cw = 512
    for j in range(S_INNER // cw):
        c0, c1 = j * cw, (j + 1) * cw
        xs_ref[:, c0:c1] = _silu(_conv_cols(ext_ref, cw_ref, cb_ref, c0, c1)).astype(BF16)
    for j in range(S_GN // cw):
        c0, c1 = S_INNER + j * cw, S_INNER + (j + 1) * cw
        bm = _silu(_conv_cols(ext_ref, cw_ref, cb_ref, c0, c1))
        bmt_ref[j * cw:(j + 1) * cw, :] = bm.T.astype(BF16)
        c0, c1 = c0 + S_GN, c1 + S_GN
        cm_ref[:, j * cw:(j + 1) * cw] = _silu(_conv_cols(ext_ref, cw_ref, cb_ref, c0, c1)).astype(BF16)
    x = dtr_ref[...] + dtb_ref[...]
    dt = jnp.maximum(x, 0.0) + jnp.log1p(jnp.exp(-jnp.abs(x)))
    dt_ref[...] = dt
    da = dt * a_ref[...]
    lower, upper = _tri_masks()
    lane_c = lax.broadcasted_iota(jnp.int32, (CHUNK, LANES), 1)
    for c in range(CPB):
        dc = da[c * CHUNK:(c + 1) * CHUNK, :]
        pref = _tri_dot(lower.astype(BF16), dc)
        suff = _tri_dot(upper.astype(BF16), dc)
        cum_ref[c * CHUNK:(c + 1) * CHUNK, :] = jnp.where(lane_c < S_HEADS, pref, suff)


def _l1_conv(xbc, conv_w, conv_b, dt_raw, dt_bias, a_neg):
    full = lambda shape: pl.BlockSpec(shape, lambda i: (0,) * len(shape))
    tok = lambda w: pl.BlockSpec((TB, w), lambda i: (i, 0))
    return pl.pallas_call(
        _l1_conv_kernel,
        grid=(NB,),
        in_specs=_halo_specs(S_CONV_CH) + [
            full((CONV_W, S_CONV_CH)), full((1, S_CONV_CH)), tok(LANES), full((1, LANES)), full((1, LANES))],
        out_specs=[tok(S_INNER), pl.BlockSpec((S_GN, TB), lambda i: (0, i)), tok(S_GN), tok(LANES), tok(LANES)],
        out_shape=[jax.ShapeDtypeStruct((N_TOK, S_INNER), BF16),
                   jax.ShapeDtypeStruct((S_GN, N_TOK), BF16),
                   jax.ShapeDtypeStruct((N_TOK, S_GN), BF16),
                   jax.ShapeDtypeStruct((N_TOK, LANES), F32),
                   jax.ShapeDtypeStruct((N_TOK, LANES), F32)],
        scratch_shapes=[pltpu.VMEM((TB + 2 * HALO, S_CONV_CH), F32)],
        compiler_params=_cparams(("arbitrary",)),
        name="l1_conv",
    )(xbc, xbc, xbc, conv_w, conv_b, dt_raw, dt_bias, a_neg)


GW = S_HPG * S_HEADDIM
GPS = 2


def _ssd_chunk(r, x, bmt, cm, dgt, dsk, st_ref, y_ref, rows, cols):
    lower, upper = _tri_masks()
    mask = lower if r == 0 else upper
    x = x.astype(F32)
    st = st_ref[r]
    cb = jnp.dot(cm, bmt, preferred_element_type=F32)
    lane = lax.broadcasted_iota(jnp.int32, (CHUNK, GW), 1)
    column = lambda row: jnp.broadcast_to(row, (CHUNK, CHUNK)).T
    ws, xd, cum_cols, dt_cols = [], [], [], []
    for j in range(S_HPG):
        c = S_HPG * r + j
        dt_row = dgt[c:c + 1, :]
        cum_row = dgt[N_DIR * S_HPG + c:N_DIR * S_HPG + c + 1, :]
        cum_cols.append(column(cum_row))
        dt_cols.append(column(dt_row))
        w = jnp.exp(jnp.where(mask, cum_cols[j] - cum_row, -jnp.inf)) * cb * dt_row
        ws.append(w.astype(BF16))
        head = jnp.logical_and(lane >= j * S_HEADDIM, lane < (j + 1) * S_HEADDIM)
        xd.append(jnp.where(head, x, 0.0).astype(BF16))
    y = jnp.dot(jnp.concatenate(ws, axis=1), jnp.concatenate(xd, axis=0), preferred_element_type=F32)

    def per_head_lanes(cols):
        lane_c = lax.broadcasted_iota(jnp.int32, (CHUNK, CHUNK), 1)
        per = CHUNK // S_HEADDIM
        parts = []
        for g in range(S_HPG // per):
            part = cols[g * per]
            for k in range(1, per):
                part = jnp.where(lane_c >= k * S_HEADDIM, cols[g * per + k], part)
            parts.append(part)
        return jnp.concatenate(parts, axis=1)

    cum_b = per_head_lanes(cum_cols)
    dt_b = per_head_lanes(dt_cols)
    y = y + jnp.exp(cum_b) * _bdot(cm, st)
    if r == 0:
        y = y + dsk * x
    y_ref[rows, cols] = y.astype(BF16)
    last_b = cum_b[CHUNK - 1:CHUNK, :] if r == 0 else cum_b[0:1, :]
    xw = x * (jnp.exp(last_b - cum_b) * dt_b)
    st_ref[r] = jnp.exp(last_b) * st + _bdot(bmt, xw)


def _ssd_kernel(xf_ref, bf_ref, cf_ref, dtf_ref, xb_ref, bb_ref, cb_ref, dtb_ref,
                dsk_ref, s0_ref, yf_ref, yb_ref, so_ref, st_ref):
    s = pl.program_id(1)

    @pl.when(_seq_start(s))
    def _():
        is_ctx = s < NB_CTX
        for gi in range(GPS):
            for r in range(N_DIR):
                s0 = jnp.concatenate([s0_ref[r, gi * S_HPG + j] for j in range(S_HPG)], axis=0)
                st_ref[gi, r] = jnp.where(is_ctx, 0.0, s0.T)

    for gi in range(GPS):
        xc = slice(gi * GW, (gi + 1) * GW)
        nc = slice(gi * S_STATE, (gi + 1) * S_STATE)
        dsk = dsk_ref[gi]
        for c in range(CPB):
            rows = slice(c * CHUNK, (c + 1) * CHUNK)
            _ssd_chunk(0, xf_ref[rows, xc], bf_ref[nc, rows], cf_ref[rows, nc], dtf_ref[gi, :, rows],
                       dsk, st_ref.at[gi], yf_ref, rows, xc)
        for c in reversed(range(CPB)):
            rows = slice(c * CHUNK, (c + 1) * CHUNK)
            _ssd_chunk(1, xb_ref[rows, xc], bb_ref[nc, rows], cb_ref[rows, nc], dtb_ref[gi, :, rows],
                       dsk, st_ref.at[gi], yb_ref, rows, xc)

    @pl.when(s < NB_CTX)
    def _():
        for gi in range(GPS):
            for r in range(N_DIR):
                stt = st_ref[gi, r].T
                for j in range(S_HPG):
                    so_ref[r, gi * S_HPG + j] = stt[j * S_HEADDIM:(j + 1) * S_HEADDIM, :]


def _ssd_scan(xs, bmt, cm, dgt, dsk, s0):
    ng = 2 * N_DIR * S_HPG
    def side(blk):
        return [
            pl.BlockSpec((TB, GPS * GW), lambda g, s: (blk(s), g)),
            pl.BlockSpec((GPS * S_STATE, TB), lambda g, s: (g, blk(s))),
            pl.BlockSpec((TB, GPS * S_STATE), lambda g, s: (blk(s), g)),
            pl.BlockSpec((GPS, ng, TB), lambda g, s: (g, 0, blk(s))),
        ]
    ident = lambda s: s
    lat = _lat_seq_of_block
    ctx = lambda s: jnp.minimum(s, NB_CTX - 1)
    st_block = (None, None, N_DIR, GPS * S_HPG, S_HEADDIM, S_STATE)
    return pl.pallas_call(
        _ssd_kernel,
        grid=(S_GROUPS // GPS, NB),
        in_specs=side(ident) + side(_bwd_block) + [
            pl.BlockSpec((GPS, 1, GW), lambda g, s: (g, 0, 0)),
            pl.BlockSpec(st_block, lambda g, s: (lat(s), 0, 0, g, 0, 0)),
        ],
        out_specs=[
            pl.BlockSpec((TB, GPS * GW), lambda g, s: (s, g)),
            pl.BlockSpec((TB, GPS * GW), lambda g, s: (_bwd_block(s), g)),
            pl.BlockSpec(st_block, lambda g, s: (ctx(s), 0, 0, g, 0, 0)),
        ],
        out_shape=[
            jax.ShapeDtypeStruct((N_TOK, S_INNER), BF16),
            jax.ShapeDtypeStruct((N_TOK, S_INNER), BF16),
            jax.ShapeDtypeStruct((BATCH, 1, N_DIR, S_HEADS, S_HEADDIM, S_STATE), F32),
        ],
        scratch_shapes=[pltpu.VMEM((GPS, N_DIR, S_STATE, GW), F32)],
        compiler_params=_cparams(("arbitrary", "arbitrary")),
        name="ssd_scan",
    )(xs, bmt, cm, dgt, xs, bmt, cm, dgt, dsk, s0)


def _l1_out_kernel(yf_ref, yb_ref, z_ref, nw_ref, w_ref, x_ref, mod_ref, lg_ref, lb_ref, wr_ref,
                   x3_ref, h4_ref, gates_ref):
    y = (yf_ref[...].astype(F32) + yb_ref[...].astype(F32)) * _silu(z_ref[...].astype(F32))
    y = y * lax.rsqrt(jnp.mean(y * y, axis=-1, keepdims=True) + RMS_EPS) * nw_ref[...]
    out = _bdot(y, w_ref[...])
    x3 = _post_ln(x_ref[...], out, mod_ref[2:3, :], lg_ref[...], lb_ref[...])
    x3_ref[...] = x3
    h4 = x3 * (1.0 + mod_ref[4:5, :]) + mod_ref[3:4, :]
    for cc in range(ROW_TILE):
        h4_ref[pl.ds(cc, TB, stride=ROW_TILE), :] = h4[:, cc * LANES:(cc + 1) * LANES]
    lane = lax.broadcasted_iota(jnp.int32, (TB, LANES), 1)
    logits = jnp.full((TB, LANES), -jnp.inf, F32)
    for e in range(N_EXPERTS):
        le = jnp.sum(h4 * wr_ref[e:e + 1, :], axis=-1, keepdims=True)
        logits = jnp.where(lane == e, le, logits)
    v1 = jnp.max(logits, axis=-1, keepdims=True)
    i1 = jnp.min(jnp.where(logits == v1, lane, LANES), axis=-1, keepdims=True)
    rest = jnp.where(lane == i1, -jnp.inf, logits)
    v2 = jnp.max(rest, axis=-1, keepdims=True)
    i2 = jnp.min(jnp.where(rest == v2, lane, LANES), axis=-1, keepdims=True)
    e2 = jnp.exp(v2 - v1)
    w1 = 1.0 / (1.0 + e2)
    w2 = e2 / (1.0 + e2)
    route = jnp.where(lane == 0, w1, jnp.where(lane == 1, w2, 0.0))
    route = jnp.where(lane == 2, i1.astype(F32), jnp.where(lane == 3, i2.astype(F32), route))
    gates_ref[...] = route


def _l1_out(yf, yb, z, norm_w, w_out, x2, mod, ln_g, ln_b, wr_t):
    full = lambda shape: pl.BlockSpec(shape, lambda i: (0,) * len(shape))
    tok = lambda w: pl.BlockSpec((TB, w), lambda i: (i, 0))
    return pl.pallas_call(
        _l1_out_kernel,
        grid=(NB,),
        in_specs=[tok(S_INNER), tok(S_INNER), tok(S_INNER), full((1, S_INNER)), full((S_INNER, D_MODEL)),
                  tok(D_MODEL), _mod_spec(1, TB), full((1, D_MODEL)), full((1, D_MODEL)),
                  full((N_EXPERTS, D_MODEL))],
        out_specs=[tok(D_MODEL), pl.BlockSpec((TB * ROW_TILE, LANES), lambda i: (i, 0)), tok(LANES)],
        out_shape=[jax.ShapeDtypeStruct((N_TOK, D_MODEL), F32),
                   jax.ShapeDtypeStruct((N_TOK * ROW_TILE, LANES), F32),
                   jax.ShapeDtypeStruct((N_TOK, LANES), F32)],
        compiler_params=_cparams(("arbitrary",)),
        name="l1_out",
    )(yf, yb, z, norm_w, w_out, x2, mod, ln_g, ln_b, wr_t)


TM_E = 512
N_ASSIGN = TOP_K * N_TOK
P_ROWS = N_ASSIGN + N_EXPERTS * TM_E
N_PT = P_ROWS // TM_E
ROW_TILE = D_MODEL // LANES
NF_MOE = E_FF // TF_MOE


def _row(i):
    return pl.ds(pl.multiple_of(i * ROW_TILE, ROW_TILE), ROW_TILE)


def _expert_kernel(te_ref, nv_ref, src_ref, srcn_ref, asg_ref, asgp_ref, h_ref, w1_ref, w3_ref, w2_ref, yc_ref,
                   xbuf, xb_ref, acc_ref, ybuf, gsem, ssem):
    j, f = pl.program_id(0), pl.program_id(1)
    n_valid = nv_ref[0]
    valid = j < n_valid
    first_f, last_f = f == 0, f == NF_MOE - 1

    def gather_row(idx_ref, r):
        pltpu.make_async_copy(h_ref.at[_row(idx_ref[0, r])], xbuf.at[_row(r)], gsem).start()

    def scatter_row(idx_ref, r):
        pltpu.make_async_copy(ybuf.at[_row(r)], yc_ref.at[_row(idx_ref[0, r])], ssem).start()

    def looped(row_fn, idx_ref):
        def body(r, carry):
            row_fn(idx_ref, r)
            return carry
        lax.fori_loop(0, TM_E, body, 0, unroll=8)

    def wait_gather():
        pltpu.make_async_copy(h_ref.at[pl.ds(0, TM_E * ROW_TILE)], xbuf, gsem).wait()

    def wait_scatter():
        pltpu.make_async_copy(ybuf, yc_ref.at[pl.ds(0, TM_E * ROW_TILE)], ssem).wait()

    def load_rows():
        for cc in range(ROW_TILE):
            xb_ref[:, cc * LANES:(cc + 1) * LANES] = xbuf[pl.ds(cc, TM_E, stride=ROW_TILE), :].astype(BF16)
        acc_ref[...] = jnp.zeros_like(acc_ref)

    def swiglu_step():
        x = xb_ref[...]
        a = jnp.dot(x, w1_ref[...], preferred_element_type=F32)
        b = jnp.dot(x, w3_ref[...], preferred_element_type=F32)
        acc_ref[...] += _bdot(_silu(a) * b, w2_ref[...])

    @pl.when(jnp.logical_and(j == 0, first_f))
    def _():
        looped(gather_row, src_ref)

    @pl.when(jnp.logical_and(first_f, j <= n_valid))
    def _():
        wait_gather()

    @pl.when(jnp.logical_and(valid, jnp.logical_and(first_f, j == 0)))
    def _():
        load_rows()
        swiglu_step()
        for r in range(TM_E):
            gather_row(srcn_ref, r)

    @pl.when(jnp.logical_and(valid, jnp.logical_and(first_f, j > 0)))
    def _():
        load_rows()
        swiglu_step()
        for r in range(TM_E):
            gather_row(srcn_ref, r)
        for r in range(TM_E):
            scatter_row(asgp_ref, r)

    @pl.when(jnp.logical_and(valid, last_f))
    def _():
        swiglu_step()

    @pl.when(jnp.logical_and(valid, last_f))
    def _():
        @pl.when(j > 0)
        def _():
            wait_scatter()

        for cc in range(ROW_TILE):
            ybuf[pl.ds(cc, TM_E, stride=ROW_TILE), :] = acc_ref[:, cc * LANES:(cc + 1) * LANES]

    @pl.when(jnp.logical_and(first_f, j == n_valid))
    def _():
        looped(scatter_row, asgp_ref)
        wait_scatter()

    @pl.when(jnp.logical_and(jnp.logical_not(valid), last_f))
    def _():
        ybuf[...] = jnp.zeros_like(ybuf)
        looped(scatter_row, asg_ref)
        wait_scatter()


def _experts(tile_expert, n_valid, src, asg, h4, w1, w3, w2):
    assert NF_MOE >= 2
    fidx = lambda j, f, nv: jnp.where(j < nv[0], f, NF_MOE - 1)
    smem_tile = lambda off: pl.BlockSpec(
        (None, 1, TM_E), lambda j, f, te, nv: (jnp.clip(j + off, 0, N_PT - 1), 0, 0), memory_space=pltpu.SMEM)
    src3 = src.reshape(N_PT, 1, TM_E)
    asg3 = asg.reshape(N_PT, 1, TM_E)
    return pl.pallas_call(
        _expert_kernel,
        grid_spec=pltpu.PrefetchScalarGridSpec(
            num_scalar_prefetch=2,
            grid=(N_PT, NF_MOE),
            in_specs=[smem_tile(0), smem_tile(1), smem_tile(0), smem_tile(-1),
                      pl.BlockSpec(memory_space=pl.ANY),
                      pl.BlockSpec((None, D_MODEL, TF_MOE), lambda j, f, te, nv: (te[j], 0, fidx(j, f, nv))),
                      pl.BlockSpec((None, D_MODEL, TF_MOE), lambda j, f, te, nv: (te[j], 0, fidx(j, f, nv))),
                      pl.BlockSpec((None, TF_MOE, D_MODEL), lambda j, f, te, nv: (te[j], fidx(j, f, nv), 0))],
            out_specs=pl.BlockSpec(memory_space=pl.ANY),
            scratch_shapes=[pltpu.VMEM((TM_E * ROW_TILE, LANES), F32),
                            pltpu.VMEM((TM_E, D_MODEL), BF16),
                            pltpu.VMEM((TM_E, D_MODEL), F32),
                            pltpu.VMEM((TM_E * ROW_TILE, LANES), F32),
                            pltpu.SemaphoreType.DMA(()),
                            pltpu.SemaphoreType.DMA(())],
        ),
        out_shape=jax.ShapeDtypeStruct((P_ROWS * ROW_TILE, LANES), F32),
        compiler_params=_cparams(("arbitrary", "arbitrary")),
        name="experts",
    )(tile_expert, n_valid, src3, src3, asg3, asg3, h4, w1, w3, w2)


def _moe_out_kernel(yc_ref, r_ref, x_ref, mod_ref, lg_ref, lb_ref, yp_ref, ys_ref, f_ref):
    i = pl.program_id(0)
    w1, w2 = r_ref[:, 0:1], r_ref[:, 1:2]
    for cc in range(ROW_TILE):
        y_top1 = yc_ref[pl.ds(cc, TM_FF, stride=TOP_K * ROW_TILE), :]
        y_top2 = yc_ref[pl.ds(ROW_TILE + cc, TM_FF, stride=TOP_K * ROW_TILE), :]
        f_ref[:, cc * LANES:(cc + 1) * LANES] = w1 * y_top1 + w2 * y_top2
    y = _post_ln(x_ref[...], f_ref[...], mod_ref[5:6, :], lg_ref[...], lb_ref[...])

    @pl.when(i < N_CTX // TM_FF)
    def _():
        yp_ref[...] = y

    @pl.when(i >= N_CTX // TM_FF)
    def _():
        ys_ref[...] = y


def _moe_out(yc, route, x3, mod, ln_g, ln_b):
    nctx = N_CTX // TM_FF
    tok = lambda w: pl.BlockSpec((TM_FF, w), lambda i: (i, 0))
    vec = pl.BlockSpec((1, D_MODEL), lambda i: (0, 0))
    return pl.pallas_call(
        _moe_out_kernel,
        grid=(N_TOK // TM_FF,),
        in_specs=[pl.BlockSpec((TM_FF * TOP_K * ROW_TILE, LANES), lambda i: (i, 0)),
                  tok(LANES), tok(D_MODEL), _mod_spec(1, TM_FF), vec, vec],
        out_specs=[pl.BlockSpec((TM_FF, D_MODEL), lambda i: (jnp.minimum(i, nctx - 1), 0)),
                   pl.BlockSpec((TM_FF, D_MODEL), lambda i: (jnp.maximum(i - nctx, 0), 0))],
        out_shape=[jax.ShapeDtypeStruct((N_CTX, D_MODEL), F32),
                   jax.ShapeDtypeStruct((N_LAT, D_MODEL), F32)],
        scratch_shapes=[pltpu.VMEM((TM_FF, D_MODEL), F32)],
        compiler_params=_cparams(("arbitrary",)),
        name="moe_out",
    )(yc, route, x3, mod, ln_g, ln_b)


def _routing_tables(route):
    flat_e = route[:, 2:2 + TOP_K].astype(jnp.int32).reshape(N_ASSIGN)
    onehot = (flat_e[:, None] == jnp.arange(N_EXPERTS, dtype=jnp.int32)[None, :]).astype(jnp.int32)
    csum = jnp.cumsum(onehot, axis=0)
    rank = jnp.take_along_axis(csum, flat_e[:, None], axis=1)[:, 0] - 1
    ntile = (csum[-1] + TM_E - 1) // TM_E
    tile_end = jnp.cumsum(ntile)
    dest = (tile_end - ntile)[flat_e] * TM_E + rank
    asg = jnp.full((P_ROWS,), -1, jnp.int32).at[dest].set(jnp.arange(N_ASSIGN, dtype=jnp.int32))
    is_pad = asg < 0
    src = jnp.maximum(asg, 0) // TOP_K
    asg = jnp.where(is_pad, N_ASSIGN - 1 + jnp.cumsum(is_pad.astype(jnp.int32)), asg)
    tiles = jnp.arange(N_PT, dtype=jnp.int32)
    n_valid = tile_end[-1]
    tile_expert = jnp.sum((jnp.minimum(tiles, n_valid - 1)[:, None] >= tile_end[None, :]).astype(jnp.int32), axis=1)
    return src, asg, tile_expert, n_valid.reshape(1)


def _grid_pos_embed():
    rows = DEC_SEQ // GRID_W
    quarter = D_MODEL // 4
    freqs = jnp.exp(-math.log(POS_BASE) * jnp.arange(quarter, dtype=F32) / quarter)
    er = jnp.arange(rows, dtype=F32)[:, None] * freqs
    ec = jnp.arange(GRID_W, dtype=F32)[:, None] * freqs
    row_emb = jnp.concatenate([jnp.sin(er), jnp.cos(er)], axis=-1)
    col_emb = jnp.concatenate([jnp.sin(ec), jnp.cos(ec)], axis=-1)
    pos = jnp.concatenate([jnp.broadcast_to(row_emb[:, None, :], (rows, GRID_W, D_MODEL // 2)),
                           jnp.broadcast_to(col_emb[None, :, :], (rows, GRID_W, D_MODEL // 2))], axis=-1)
    return pos.reshape(DEC_SEQ, D_MODEL)


def _pad_lanes(a, width=LANES):
    return jnp.pad(a, [(0, 0)] * (a.ndim - 1) + [(0, width - a.shape[-1])])


def kernel(x_prompt, x_sample, c, state_mlstm_C, state_mlstm_n, state_mlstm_m, state_ssm, c_ctx,
           ada_w, ada_b, ln_g, ln_b,
           ml_w_up, ml_conv_w, ml_conv_b, ml_wq, ml_wk, ml_wv, ml_w_ig, ml_b_ig, ml_w_fg, ml_b_fg,
           ml_norm_w, ml_skip, ml_w_down,
           ss_w_in, ss_conv_w, ss_conv_b, ss_dt_bias, ss_a_log, ss_d, ss_norm_w, ss_w_out,
           ff_w1, ff_w3, ff_w2,
           moe_w_router, moe_w1, moe_w3, moe_w2):
    row = lambda a: a.reshape(1, -1)
    cond = jnp.concatenate([c_ctx[None, :], c, jnp.zeros((COND_ROWS - N_COND, D_MODEL), F32)], axis=0)
    mod = _ada_mod(cond, ada_w.astype(BF16), ada_b)

    xa, xm, z0 = _l0_in(x_prompt.reshape(N_CTX, D_MODEL), x_sample.reshape(N_LAT, D_MODEL), _grid_pos_embed(),
                        mod, ml_w_up[0].astype(BF16))
    ng = N_DIR * M_HEADS
    wg = jnp.concatenate([ml_w_ig[0].transpose(1, 0, 2).reshape(3 * M_INNER, ng),
                          ml_w_fg[0].transpose(1, 0, 2).reshape(3 * M_INNER, ng)], axis=1)
    bg = jnp.concatenate([ml_b_ig[0].reshape(ng), ml_b_fg[0].reshape(ng)])
    xc, q, k, v, g, bc = _l0_qkv(xm, ml_conv_w[0], row(ml_conv_b[0]),
                                 ml_wq[0].astype(BF16), ml_wk[0].astype(BF16), ml_wv[0].astype(BF16),
                                 _pad_lanes(wg).astype(BF16), row(_pad_lanes(bg)))
    gh = jnp.concatenate([g[:, :ng].reshape(N_TOK, N_DIR, M_HEADS),
                          bc[:, ng:2 * ng].reshape(N_TOK, N_DIR, M_HEADS)], axis=1)
    ght = gh.transpose(2, 1, 0)
    gh = gh.transpose(2, 0, 1)
    hf, hb, new_c, new_n, new_m = _mlstm_scan(
        q, k, v, gh, ght, state_mlstm_C,
        state_mlstm_n.reshape(DEC_BATCH, N_DIR, M_HEADS, 1, M_DH),
        state_mlstm_m.reshape(DEC_BATCH, N_DIR, M_HEADS, 1, 1))
    x1, h2 = _l0_out(hf, hb, xc, z0, row(ml_norm_w[0]), row(ml_skip[0]), ml_w_down[0].astype(BF16),
                     xa, mod, row(ln_g[0, 0]), row(ln_b[0, 0]))
    x2, h3 = _ffn(h2, ff_w1[0].astype(BF16), ff_w3[0].astype(BF16), ff_w2[0].astype(BF16),
                  x1, mod, row(ln_g[0, 1]), row(ln_b[0, 1]))

    w_in = ss_w_in[0]
    z1, xbc, dt_raw = _l1_in(h3, w_in[:, :S_INNER].astype(BF16),
                             w_in[:, S_INNER:S_INNER + S_CONV_CH].astype(BF16),
                             _pad_lanes(w_in[:, S_INNER + S_CONV_CH:]).astype(BF16))
    a_neg = -jnp.exp(ss_a_log[0].reshape(N_DIR * S_HEADS))
    xs, bmt, cm, dt, cum = _l1_conv(xbc, ss_conv_w[0], row(ss_conv_b[0]), dt_raw,
                                    row(_pad_lanes(ss_dt_bias[0].reshape(N_DIR * S_HEADS))), row(_pad_lanes(a_neg)))
    nh = N_DIR * S_HEADS
    to_group = lambda a: a[:, :nh].reshape(N_TOK, N_DIR, S_GROUPS, S_HPG).transpose(2, 0, 1, 3).reshape(
        S_GROUPS, N_TOK, N_DIR * S_HPG)
    dgt = jnp.concatenate([to_group(dt), to_group(cum)], axis=-1).transpose(0, 2, 1)
    dsk = jnp.repeat(ss_d[0], S_HEADDIM).reshape(S_GROUPS, 1, GW)
    yf, yb, new_s = _ssd_scan(xs, bmt, cm, dgt, dsk, state_ssm)
    x3, h4, route = _l1_out(yf, yb, z1, row(ss_norm_w[0]), ss_w_out[0].astype(BF16), x2, mod,
                            row(ln_g[1, 0]), row(ln_b[1, 0]), moe_w_router[0].T)
    src, asg, tile_expert, n_valid = _routing_tables(route)
    yc = _experts(tile_expert, n_valid, src, asg, h4,
                  moe_w1[0].astype(BF16), moe_w3[0].astype(BF16), moe_w2[0].astype(BF16))
    yp, ys = _moe_out(yc, route, x3, mod, row(ln_g[1, 1]), row(ln_b[1, 1]))

    return (yp.reshape(BATCH, SEQ, D_MODEL), ys.reshape(DEC_BATCH, DEC_SEQ, D_MODEL),
            new_c,
            new_n.reshape(BATCH, 1, N_DIR, M_HEADS, M_DH),
            new_m.reshape(BATCH, 1, N_DIR, M_HEADS),
            new_s)
```

```python
import functools
import math

import jax
import jax.numpy as jnp
from jax import lax
from jax.experimental import pallas as pl
from jax.experimental.pallas import tpu as pltpu

F32 = jnp.float32
BF16 = jnp.bfloat16

D_MODEL = 1024
BATCH = 16
SEQ = 256
DEC_BATCH = 4
DEC_SEQ = 4096
GRID_W = 64
CHUNK = 128
CONV_W = 5
N_DIR = 2
M_INNER = 2 * D_MODEL
M_HEADS = 4
M_DH = M_INNER // M_HEADS
S_INNER = 2 * D_MODEL
S_HEADDIM = 64
S_HEADS = S_INNER // S_HEADDIM
S_GROUPS = 8
S_HPG = S_HEADS // S_GROUPS
S_STATE = 128
S_GN = S_GROUPS * S_STATE
S_CONV_CH = S_INNER + 2 * S_GN
D_FF = 2816
N_EXPERTS = 8
TOP_K = 2
E_FF = 3584
DEPTH = 2
ALPHA = (2 * DEPTH) ** 0.25
LN_EPS = 1e-5
RMS_EPS = 1e-5
POS_BASE = 10000.0

N_CTX = BATCH * SEQ
N_LAT = DEC_BATCH * DEC_SEQ
N_TOK = N_CTX + N_LAT
N_COND = 1 + DEC_BATCH
COND_ROWS = 8

TB = 256
NB = N_TOK // TB
NB_CTX = N_CTX // TB
NB_SEQ = DEC_SEQ // TB
HALO = 8
CPB = TB // CHUNK
MCH = TB
C_SLAB = 256

TM_FF = 512
TF_FF = 1408
TF_MOE = 896
LANES = 128

VMEM_LIMIT = 56 * 1024 * 1024


def _cparams(sem):
    return pltpu.CompilerParams(dimension_semantics=sem, vmem_limit_bytes=VMEM_LIMIT)


def _cond_of_token(t0):
    return jnp.where(t0 < N_CTX, 0, 1 + (t0 - N_CTX) // DEC_SEQ)


def _lat_seq_of_block(s):
    return jnp.maximum(s - NB_CTX, 0) // NB_SEQ


def _bwd_block(s):
    k = s - NB_CTX
    return jnp.where(s < NB_CTX, s, NB_CTX + (k // NB_SEQ) * NB_SEQ + (NB_SEQ - 1 - k % NB_SEQ))


def _seq_start(s):
    return jnp.logical_or(s < NB_CTX, (s - NB_CTX) % NB_SEQ == 0)


def _seq_end(s):
    return jnp.logical_or(s < NB_CTX, (s - NB_CTX) % NB_SEQ == NB_SEQ - 1)


def _silu(x):
    return x * jax.nn.sigmoid(x)


def _bdot(a, b):
    return jnp.dot(a.astype(BF16), b.astype(BF16), preferred_element_type=F32)


def _tri_dot(tri, x):
    hi = x.astype(BF16)
    r1 = x - hi.astype(F32)
    mid = r1.astype(BF16)
    lo = (r1 - mid.astype(F32)).astype(BF16)
    d = lambda a: jnp.dot(tri, a, preferred_element_type=F32)
    return d(hi) + d(mid) + d(lo)


def _tri_masks(n=CHUNK):
    t = lax.broadcasted_iota(jnp.int32, (n, n), 0)
    u = lax.broadcasted_iota(jnp.int32, (n, n), 1)
    return u <= t, u >= t


def _layer_norm(r, g, b):
    mu = jnp.mean(r, axis=-1, keepdims=True)
    d = r - mu
    var = jnp.mean(d * d, axis=-1, keepdims=True)
    return d * lax.rsqrt(var + LN_EPS) * g + b


def _ada_kernel(cond_ref, w_ref, b_ref, o_ref):
    o_ref[...] = _bdot(_silu(cond_ref[...]), w_ref[...]) + b_ref[...]


def _ada_mod(cond, ada_w, ada_b):
    ncol = 6 * D_MODEL // D_MODEL
    out = pl.pallas_call(
        _ada_kernel,
        grid=(DEPTH, ncol),
        in_specs=[
            pl.BlockSpec((COND_ROWS, D_MODEL), lambda l, j: (0, 0)),
            pl.BlockSpec((None, D_MODEL, D_MODEL), lambda l, j: (l, 0, j)),
            pl.BlockSpec((None, 1, D_MODEL), lambda l, j: (l, 0, j)),
        ],
        out_specs=pl.BlockSpec((None, COND_ROWS, D_MODEL), lambda l, j: (l, 0, j)),
        out_shape=jax.ShapeDtypeStruct((DEPTH, COND_ROWS, 6 * D_MODEL), F32),
        compiler_params=_cparams(("arbitrary", "arbitrary")),
        name="ada_mod",
    )(cond, ada_w, ada_b.reshape(DEPTH, 1, 6 * D_MODEL))
    return out.reshape(DEPTH, COND_ROWS, 6, D_MODEL)


def _mod_spec(layer, tm):
    return pl.BlockSpec((None, None, 6, D_MODEL), lambda i, *_: (layer, _cond_of_token(i * tm), 0, 0))


def _l0_in_kernel(xp_ref, xs_ref, pos_ref, mod_ref, w_ref, xa_ref, xm_ref, z_ref):
    i = pl.program_id(0)
    x = jnp.where(i >= NB_CTX, xs_ref[...] + pos_ref[...], xp_ref[...])
    xa_ref[...] = x
    h = x * (1.0 + mod_ref[1:2, :]) + mod_ref[0:1, :]
    u = _bdot(h, w_ref[...])
    xm_ref[...] = u[:, :M_INNER]
    z_ref[...] = u[:, M_INNER:].astype(BF16)


def _l0_in(xp, xs, pos, mod, w_up):
    return pl.pallas_call(
        _l0_in_kernel,
        grid=(NB,),
        in_specs=[
            pl.BlockSpec((TB, D_MODEL), lambda i: (jnp.minimum(i, NB_CTX - 1), 0)),
            pl.BlockSpec((TB, D_MODEL), lambda i: (jnp.maximum(i - NB_CTX, 0), 0)),
            pl.BlockSpec((TB, D_MODEL), lambda i: (jnp.maximum(i - NB_CTX, 0) % NB_SEQ, 0)),
            _mod_spec(0, TB),
            pl.BlockSpec((D_MODEL, 2 * M_INNER), lambda i: (0, 0)),
        ],
        out_specs=[
            pl.BlockSpec((TB, D_MODEL), lambda i: (i, 0)),
            pl.BlockSpec((TB, M_INNER), lambda i: (i, 0)),
            pl.BlockSpec((TB, M_INNER), lambda i: (i, 0)),
        ],
        out_shape=[
            jax.ShapeDtypeStruct((N_TOK, D_MODEL), F32),
            jax.ShapeDtypeStruct((N_TOK, M_INNER), F32),
            jax.ShapeDtypeStruct((N_TOK, M_INNER), BF16),
        ],
        compiler_params=_cparams(("arbitrary",)),
        name="l0_in",
    )(xp, xs, pos, mod, w_up)


def _fill_ext(ext_ref, prev_ref, main_ref, next_ref):
    i = pl.program_id(0)
    ext_ref[0:HALO, :] = jnp.where(_seq_start(i), 0.0, prev_ref[...])
    ext_ref[HALO:HALO + TB, :] = main_ref[...]
    ext_ref[HALO + TB:HALO + TB + HALO, :] = jnp.where(_seq_end(i), 0.0, next_ref[...])


def _conv_cols(ext_ref, w_ref, b_ref, c0, c1):
    acc = b_ref[:, c0:c1]
    for j in range(CONV_W):
        r0 = HALO - CONV_W // 2 + j
        acc = acc + w_ref[j:j + 1, c0:c1] * ext_ref[r0:r0 + TB, c0:c1]
    return acc


def _halo_specs(width):
    nh = TB // HALO
    last = N_TOK // HALO - 1
    return [
        pl.BlockSpec((HALO, width), lambda i: (jnp.maximum(i * nh - 1, 0), 0)),
        pl.BlockSpec((TB, width), lambda i: (i, 0)),
        pl.BlockSpec((HALO, width), lambda i: (jnp.minimum((i + 1) * nh, last), 0)),
    ]


def _l0_qkv_kernel(prev_ref, xm_ref, next_ref, cw_ref, cb_ref, wq_ref, wk_ref, wv_ref, wg_ref, bg_ref,
                   xc_ref, q_ref, k_ref, v_ref, g_ref, bc_ref, ext_ref):
    _fill_ext(ext_ref, prev_ref, xm_ref, next_ref)
    pre = jnp.zeros((TB, LANES), F32) + bg_ref[...]
    for h in range(M_HEADS):
        c0, c1 = h * M_DH, (h + 1) * M_DH
        xc = _silu(_conv_cols(ext_ref, cw_ref, cb_ref, c0, c1))
        xcb = xc.astype(BF16)
        xc_ref[:, c0:c1] = xcb
        q = jnp.dot(xcb, wq_ref[h], preferred_element_type=F32).astype(BF16)
        k = (jnp.dot(xcb, wk_ref[h], preferred_element_type=F32) * (M_DH ** -0.5)).astype(BF16)
        v = jnp.dot(xm_ref[:, c0:c1].astype(BF16), wv_ref[h], preferred_element_type=F32).astype(BF16)
        q_ref[:, c0:c1] = q
        k_ref[:, c0:c1] = k
        v_ref[:, c0:c1] = v
        pre = pre + jnp.dot(q, wg_ref[c0:c1, :], preferred_element_type=F32)
        pre = pre + jnp.dot(k, wg_ref[M_INNER + c0:M_INNER + c1, :], preferred_element_type=F32)
        pre = pre + jnp.dot(v, wg_ref[2 * M_INNER + c0:2 * M_INNER + c1, :], preferred_element_type=F32)
    lane = lax.broadcasted_iota(jnp.int32, (TB, LANES), 1)
    ng = N_DIR * M_HEADS
    logsig = -(jnp.maximum(-pre, 0.0) + jnp.log1p(jnp.exp(-jnp.abs(pre))))
    g = jnp.where(jnp.logical_and(lane >= ng, lane < 2 * ng), logsig, pre)
    g_ref[...] = g
    lower, upper = _tri_masks(MCH)
    lane_c = lax.broadcasted_iota(jnp.int32, (MCH, LANES), 1)
    fwd_lane = jnp.logical_and(lane_c >= ng, lane_c < ng + M_HEADS)
    for c in range(TB // MCH):
        gc = g[c * MCH:(c + 1) * MCH, :]
        pref = _tri_dot(lower.astype(BF16), gc)
        suff = _tri_dot(upper.astype(BF16), gc)
        bc_ref[c * MCH:(c + 1) * MCH, :] = jnp.where(fwd_lane, pref, suff)


def _l0_qkv(xm, conv_w, conv_b, wq, wk, wv, wg, bg):
    full = lambda shape: pl.BlockSpec(shape, lambda i: (0,) * len(shape))
    tok = lambda w: pl.BlockSpec((TB, w), lambda i: (i, 0))
    return pl.pallas_call(
        _l0_qkv_kernel,
        grid=(NB,),
        in_specs=_halo_specs(M_INNER) + [
            full((CONV_W, M_INNER)), full((1, M_INNER)),
            full((M_HEADS, M_DH, M_DH)), full((M_HEADS, M_DH, M_DH)), full((M_HEADS, M_DH, M_DH)),
            full((3 * M_INNER, LANES)), full((1, LANES)),
        ],
        out_specs=[tok(M_INNER), tok(M_INNER), tok(M_INNER), tok(M_INNER), tok(LANES), tok(LANES)],
        out_shape=[
            jax.ShapeDtypeStruct((N_TOK, M_INNER), BF16),
            jax.ShapeDtypeStruct((N_TOK, M_INNER), BF16),
            jax.ShapeDtypeStruct((N_TOK, M_INNER), BF16),
            jax.ShapeDtypeStruct((N_TOK, M_INNER), BF16),
            jax.ShapeDtypeStruct((N_TOK, LANES), F32),
            jax.ShapeDtypeStruct((N_TOK, LANES), F32),
        ],
        scratch_shapes=[pltpu.VMEM((TB + 2 * HALO, M_INNER), F32)],
        compiler_params=_cparams(("arbitrary",)),
        name="l0_qkv",
    )(xm, xm, xm, conv_w, conv_b, wq, wk, wv, wg, bg)


def _mlstm_chunk(r, q, k, v, gh, ght, c_ref, n_ref, m_ref):
    n = q.shape[0]
    lower, upper = _tri_masks(n)
    mask = lower if r == 0 else upper
    li_col, li_row = gh[:, r:r + 1], ght[r:r + 1, :]
    b_col, b_row = gh[:, 2 + r:3 + r], ght[2 + r:3 + r, :]
    m_prev = m_ref[r]
    n_prev = n_ref[r]
    d = jnp.where(mask, b_col - b_row + li_row, -jnp.inf)
    g = b_col + m_prev
    m_t = jnp.maximum(g, jnp.max(d, axis=1, keepdims=True))
    qk = lax.dot_general(q, k, (((1,), (1,)), ((), ())), preferred_element_type=F32)
    s = qk * jnp.exp(d - m_t)
    inter = jnp.exp(g - m_t)
    den = jnp.sum(s, axis=1, keepdims=True) + inter * jnp.sum(q.astype(F32) * n_prev, axis=1, keepdims=True)
    den = jnp.maximum(jnp.abs(den), jnp.exp(-m_t))
    s = s.astype(BF16)
    b_last = b_col[n - 1:n, :] if r == 0 else b_col[0:1, :]
    ld = b_last - b_col + li_col
    m_new = jnp.maximum(b_last + m_prev, jnp.max(ld, axis=0, keepdims=True))
    wk = jnp.exp(ld - m_new)
    decay = jnp.exp(b_last + m_prev - m_new)
    kw = k.astype(F32) * wk
    n_ref[r] = decay * n_prev + jnp.sum(kw, axis=0, keepdims=True)
    m_ref[r] = m_new
    kwt = kw.T.astype(BF16)
    hs = []
    for e in range(M_DH // C_SLAB):
        cols = slice(e * C_SLAB, (e + 1) * C_SLAB)
        c_slab = c_ref[r, :, cols]
        num = jnp.dot(s, v[:, cols], preferred_element_type=F32) + inter * _bdot(q, c_slab)
        hs.append((num / den).astype(BF16))
        c_ref[r, :, cols] = decay * c_slab + jnp.dot(kwt, v[:, cols], preferred_element_type=F32)
    return jnp.concatenate(hs, axis=1)


def _mlstm_kernel(qf_ref, kf_ref, vf_ref, ghf_ref, ghtf_ref, qb_ref, kb_ref, vb_ref, ghb_ref, ghtb_ref,
                  c0_ref, n0_ref, m0_ref,
                  hf_ref, hb_ref, co_ref, no_ref, mo_ref,
                  c_ref, n_ref, m_ref):
    s = pl.program_id(1)

    @pl.when(_seq_start(s))
    def _():
        is_ctx = s < NB_CTX
        c_ref[...] = jnp.where(is_ctx, 0.0, c0_ref[...])
        n_ref[...] = jnp.where(is_ctx, 0.0, n0_ref[...])
        m_ref[...] = jnp.where(is_ctx, 0.0, m0_ref[...])

    for c in range(TB // MCH):
        rows = slice(c * MCH, (c + 1) * MCH)
        hf_ref[rows, :] = _mlstm_chunk(0, qf_ref[rows, :], kf_ref[rows, :], vf_ref[rows, :],
                                       ghf_ref[rows, :], ghtf_ref[:, rows], c_ref, n_ref, m_ref)
    for c in reversed(range(TB // MCH)):
        rows = slice(c * MCH, (c + 1) * MCH)
        hb_ref[rows, :] = _mlstm_chunk(1, qb_ref[rows, :], kb_ref[rows, :], vb_ref[rows, :],
                                       ghb_ref[rows, :], ghtb_ref[:, rows], c_ref, n_ref, m_ref)

    @pl.when(s < NB_CTX)
    def _():
        co_ref[...] = c_ref[...]
        no_ref[...] = n_ref[...]
        mo_ref[...] = m_ref[...]


def _mlstm_scan(q, k, v, gh, ght, c0, n0, m0):
    fwd = lambda h, s: (s, h)
    bwd = lambda h, s: (_bwd_block(s), h)
    tokf = pl.BlockSpec((TB, M_DH), fwd)
    tokb = pl.BlockSpec((TB, M_DH), bwd)
    ghf = pl.BlockSpec((None, TB, 4), lambda h, s: (h, s, 0))
    ghb = pl.BlockSpec((None, TB, 4), lambda h, s: (h, _bwd_block(s), 0))
    ghtf = pl.BlockSpec((None, 4, TB), lambda h, s: (h, 0, s))
    ghtb = pl.BlockSpec((None, 4, TB), lambda h, s: (h, 0, _bwd_block(s)))
    lat = _lat_seq_of_block
    ctx = lambda s: jnp.minimum(s, NB_CTX - 1)
    return pl.pallas_call(
        _mlstm_kernel,
        grid=(M_HEADS, NB),
        in_specs=[
            tokf, tokf, tokf, ghf, ghtf, tokb, tokb, tokb, ghb, ghtb,
            pl.BlockSpec((None, None, N_DIR, None, M_DH, M_DH), lambda h, s: (lat(s), 0, 0, h, 0, 0)),
            pl.BlockSpec((None, N_DIR, None, 1, M_DH), lambda h, s: (lat(s), 0, h, 0, 0)),
            pl.BlockSpec((None, N_DIR, None, 1, 1), lambda h, s: (lat(s), 0, h, 0, 0)),
        ],
        out_specs=[
            tokf, tokb,
            pl.BlockSpec((None, None, N_DIR, None, M_DH, M_DH), lambda h, s: (ctx(s), 0, 0, h, 0, 0)),
            pl.BlockSpec((None, N_DIR, None, 1, M_DH), lambda h, s: (ctx(s), 0, h, 0, 0)),
            pl.BlockSpec((None, N_DIR, None, 1, 1), lambda h, s: (ctx(s), 0, h, 0, 0)),
        ],
        out_shape=[
            jax.ShapeDtypeStruct((N_TOK, M_INNER), BF16),
            jax.ShapeDtypeStruct((N_TOK, M_INNER), BF16),
            jax.ShapeDtypeStruct((BATCH, 1, N_DIR, M_HEADS, M_DH, M_DH), F32),
            jax.ShapeDtypeStruct((BATCH, N_DIR, M_HEADS, 1, M_DH), F32),
            jax.ShapeDtypeStruct((BATCH, N_DIR, M_HEADS, 1, 1), F32),
        ],
        scratch_shapes=[
            pltpu.VMEM((N_DIR, M_DH, M_DH), F32),
            pltpu.VMEM((N_DIR, 1, M_DH), F32),
            pltpu.VMEM((N_DIR, 1, 1), F32),
        ],
        compiler_params=_cparams(("arbitrary", "arbitrary")),
        name="mlstm_scan",
    )(q, k, v, gh, ght, q, k, v, gh, ght, c0, n0, m0)


def _post_ln(x, branch, gate, ln_g, ln_b):
    return _layer_norm(ALPHA * x + gate * branch, ln_g, ln_b)


def _l0_out_kernel(hf_ref, hb_ref, xc_ref, z_ref, nw_ref, sk_ref, w_ref, xa_ref, mod_ref, lg_ref, lb_ref,
                   x1_ref, h2_ref, o_scr):
    for h in range(M_HEADS):
        c0, c1 = h * M_DH, (h + 1) * M_DH
        hs = hf_ref[:, c0:c1].astype(F32) + hb_ref[:, c0:c1].astype(F32)
        mu = jnp.mean(hs, axis=-1, keepdims=True)
        d = hs - mu
        var = jnp.mean(d * d, axis=-1, keepdims=True)
        hn = d * lax.rsqrt(var + LN_EPS) * nw_ref[:, c0:c1]
        o = (hn + sk_ref[:, c0:c1] * xc_ref[:, c0:c1].astype(F32)) * _silu(z_ref[:, c0:c1].astype(F32))
        o_scr[:, c0:c1] = o.astype(BF16)
    out = jnp.dot(o_scr[...], w_ref[...], preferred_element_type=F32)
    x1 = _post_ln(xa_ref[...], out, mod_ref[2:3, :], lg_ref[...], lb_ref[...])
    x1_ref[...] = x1
    h2_ref[...] = (x1 * (1.0 + mod_ref[4:5, :]) + mod_ref[3:4, :]).astype(BF16)


def _l0_out(hf, hb, xc, z, norm_w, skip, w_down, xa, mod, ln_g, ln_b):
    full = lambda shape: pl.BlockSpec(shape, lambda i: (0,) * len(shape))
    tok = lambda w: pl.BlockSpec((TB, w), lambda i: (i, 0))
    return pl.pallas_call(
        _l0_out_kernel,
        grid=(NB,),
        in_specs=[tok(M_INNER), tok(M_INNER), tok(M_INNER), tok(M_INNER),
                  full((1, M_INNER)), full((1, M_INNER)), full((M_INNER, D_MODEL)),
                  tok(D_MODEL), _mod_spec(0, TB), full((1, D_MODEL)), full((1, D_MODEL))],
        out_specs=[tok(D_MODEL), tok(D_MODEL)],
        out_shape=[jax.ShapeDtypeStruct((N_TOK, D_MODEL), F32),
                   jax.ShapeDtypeStruct((N_TOK, D_MODEL), BF16)],
        scratch_shapes=[pltpu.VMEM((TB, M_INNER), BF16)],
        compiler_params=_cparams(("arbitrary",)),
        name="l0_out",
    )(hf, hb, xc, z, norm_w, skip, w_down, xa, mod, ln_g, ln_b)


def _ffn_kernel(h_ref, w1_ref, w3_ref, w2_ref, x_ref, mod_ref, lg_ref, lb_ref, modn_ref,
                x2_ref, h3_ref, acc_ref):
    f = pl.program_id(1)

    @pl.when(f == 0)
    def _():
        acc_ref[...] = jnp.zeros_like(acc_ref)

    h = h_ref[...]
    a = jnp.dot(h, w1_ref[...], preferred_element_type=F32)
    b = jnp.dot(h, w3_ref[...], preferred_element_type=F32)
    acc_ref[...] += _bdot(_silu(a) * b, w2_ref[...])

    @pl.when(f == pl.num_programs(1) - 1)
    def _():
        x2 = _post_ln(x_ref[...], acc_ref[...], mod_ref[5:6, :], lg_ref[...], lb_ref[...])
        x2_ref[...] = x2
        h3_ref[...] = (x2 * (1.0 + modn_ref[1:2, :]) + modn_ref[0:1, :]).astype(BF16)


def _ffn(h2, w1, w3, w2, x1, mod, ln_g, ln_b):
    tok = lambda w: pl.BlockSpec((TM_FF, w), lambda i, f: (i, 0))
    vec = pl.BlockSpec((1, D_MODEL), lambda i, f: (0, 0))
    return pl.pallas_call(
        _ffn_kernel,
        grid=(N_TOK // TM_FF, D_FF // TF_FF),
        in_specs=[tok(D_MODEL),
                  pl.BlockSpec((D_MODEL, TF_FF), lambda i, f: (0, f)),
                  pl.BlockSpec((D_MODEL, TF_FF), lambda i, f: (0, f)),
                  pl.BlockSpec((TF_FF, D_MODEL), lambda i, f: (f, 0)),
                  tok(D_MODEL), _mod_spec(0, TM_FF), vec, vec, _mod_spec(1, TM_FF)],
        out_specs=[tok(D_MODEL), tok(D_MODEL)],
        out_shape=[jax.ShapeDtypeStruct((N_TOK, D_MODEL), F32),
                   jax.ShapeDtypeStruct((N_TOK, D_MODEL), BF16)],
        scratch_shapes=[pltpu.VMEM((TM_FF, D_MODEL), F32)],
        compiler_params=_cparams(("arbitrary", "arbitrary")),
        name="ffn",
    )(h2, w1, w3, w2, x1, mod, ln_g, ln_b, mod)


def _l1_in_kernel(h_ref, wz_ref, wx_ref, wd_ref, z_ref, xbc_ref, dt_ref):
    h = h_ref[...]
    z_ref[...] = jnp.dot(h, wz_ref[...], preferred_element_type=F32).astype(BF16)
    xbc_ref[...] = jnp.dot(h, wx_ref[...], preferred_element_type=F32)
    dt_ref[...] = jnp.dot(h, wd_ref[...], preferred_element_type=F32)


def _l1_in(h3, wz, wx, wd):
    full = lambda shape: pl.BlockSpec(shape, lambda i: (0,) * len(shape))
    tok = lambda w: pl.BlockSpec((TB, w), lambda i: (i, 0))
    return pl.pallas_call(
        _l1_in_kernel,
        grid=(NB,),
        in_specs=[tok(D_MODEL), full((D_MODEL, S_INNER)), full((D_MODEL, S_CONV_CH)), full((D_MODEL, LANES))],
        out_specs=[tok(S_INNER), tok(S_CONV_CH), tok(LANES)],
        out_shape=[jax.ShapeDtypeStruct((N_TOK, S_INNER), BF16),
                   jax.ShapeDtypeStruct((N_TOK, S_CONV_CH), F32),
                   jax.ShapeDtypeStruct((N_TOK, LANES), F32)],
        compiler_params=_cparams(("arbitrary",)),
        name="l1_in",
    )(h3, wz, wx, wd)


def _l1_conv_kernel(prev_ref, xbc_ref, next_ref, cw_ref, cb_ref, dtr_ref, dtb_ref, a_ref,
                    xs_ref, bmt_ref, cm_ref, dt_ref, cum_ref, ext_ref):
    _fill_ext(ext_ref, prev_ref, xbc_ref, next_ref)
    cw = 512
    for j in range(S_INNER // cw):
        c0, c1 = j * cw, (j + 1) * cw
        xs_ref[:, c0:c1] = _silu(_conv_cols(ext_ref, cw_ref, cb_ref, c0, c1)).astype(BF16)
    for j in range(S_GN // cw):
        c0, c1 = S_INNER + j * cw, S_INNER + (j + 1) * cw
        bm = _silu(_conv_cols(ext_ref, cw_ref, cb_ref, c0, c1))
        bmt_ref[j * cw:(j + 1) * cw, :] = bm.T.astype(BF16)
        c0, c1 = c0 + S_GN, c1 + S_GN
        cm_ref[:, j * cw:(j + 1) * cw] = _silu(_conv_cols(ext_ref, cw_ref, cb_ref, c0, c1)).astype(BF16)
    x = dtr_ref[...] + dtb_ref[...]
    dt = jnp.maximum(x, 0.0) + jnp.log1p(jnp.exp(-jnp.abs(x)))
    dt_ref[...] = dt
    da = dt * a_ref[...]
    lower, upper = _tri_masks()
    lane_c = lax.broadcasted_iota(jnp.int32, (CHUNK, LANES), 1)
    for c in range(CPB):
        dc = da[c * CHUNK:(c + 1) * CHUNK, :]
        pref = _tri_dot(lower.astype(BF16), dc)
        suff = _tri_dot(upper.astype(BF16), dc)
        cum_ref[c * CHUNK:(c + 1) * CHUNK, :] = jnp.where(lane_c < S_HEADS, pref, suff)


def _l1_conv(xbc, conv_w, conv_b, dt_raw, dt_bias, a_neg):
    full = lambda shape: pl.BlockSpec(shape, lambda i: (0,) * len(shape))
    tok = lambda w: pl.BlockSpec((TB, w), lambda i: (i, 0))
    return pl.pallas_call(
        _l1_conv_kernel,
        grid=(NB,),
        in_specs=_halo_specs(S_CONV_CH) + [
            full((CONV_W, S_CONV_CH)), full((1, S_CONV_CH)), tok(LANES), full((1, LANES)), full((1, LANES))],
        out_specs=[tok(S_INNER), pl.BlockSpec((S_GN, TB), lambda i: (0, i)), tok(S_GN), tok(LANES), tok(LANES)],
        out_shape=[jax.ShapeDtypeStruct((N_TOK, S_INNER), BF16),
                   jax.ShapeDtypeStruct((S_GN, N_TOK), BF16),
                   jax.ShapeDtypeStruct((N_TOK, S_GN), BF16),
                   jax.ShapeDtypeStruct((N_TOK, LANES), F32),
                   jax.ShapeDtypeStruct((N_TOK, LANES), F32)],
        scratch_shapes=[pltpu.VMEM((TB + 2 * HALO, S_CONV_CH), F32)],
        compiler_params=_cparams(("arbitrary",)),
        name="l1_conv",
    )(xbc, xbc, xbc, conv_w, conv_b, dt_raw, dt_bias, a_neg)


GW = S_HPG * S_HEADDIM
GPS = 4


def _ssd_chunk(r, x, bmt, cm, dgt, dsk, st_ref, y_ref, rows, cols):
    lower, upper = _tri_masks()
    mask = lower if r == 0 else upper
    x = x.astype(F32)
    st = st_ref[r]
    cb = jnp.dot(cm, bmt, preferred_element_type=F32)
    lane = lax.broadcasted_iota(jnp.int32, (CHUNK, GW), 1)
    column = lambda row: jnp.broadcast_to(row, (CHUNK, CHUNK)).T
    ws, xd, cum_cols, dt_cols = [], [], [], []
    for j in range(S_HPG):
        c = S_HPG * r + j
        dt_row = dgt[c:c + 1, :]
        cum_row = dgt[N_DIR * S_HPG + c:N_DIR * S_HPG + c + 1, :]
        cum_cols.append(column(cum_row))
        dt_cols.append(column(dt_row))
        w = jnp.exp(jnp.where(mask, cum_cols[j] - cum_row, -jnp.inf)) * cb * dt_row
        ws.append(w.astype(BF16))
        head = jnp.logical_and(lane >= j * S_HEADDIM, lane < (j + 1) * S_HEADDIM)
        xd.append(jnp.where(head, x, 0.0).astype(BF16))
    y = jnp.dot(jnp.concatenate(ws, axis=1), jnp.concatenate(xd, axis=0), preferred_element_type=F32)

    def per_head_lanes(cols):
        lane_c = lax.broadcasted_iota(jnp.int32, (CHUNK, CHUNK), 1)
        per = CHUNK // S_HEADDIM
        parts = []
        for g in range(S_HPG // per):
            part = cols[g * per]
            for k in range(1, per):
                part = jnp.where(lane_c >= k * S_HEADDIM, cols[g * per + k], part)
            parts.append(part)
        return jnp.concatenate(parts, axis=1)

    cum_b = per_head_lanes(cum_cols)
    dt_b = per_head_lanes(dt_cols)
    y = y + jnp.exp(cum_b) * _bdot(cm, st)
    if r == 0:
        y = y + dsk * x
    y_ref[rows, cols] = y.astype(BF16)
    last_b = cum_b[CHUNK - 1:CHUNK, :] if r == 0 else cum_b[0:1, :]
    xw = x * (jnp.exp(last_b - cum_b) * dt_b)
    st_ref[r] = jnp.exp(last_b) * st + _bdot(bmt, xw)


def _ssd_kernel(xf_ref, bf_ref, cf_ref, dtf_ref, xb_ref, bb_ref, cb_ref, dtb_ref,
                dsk_ref, s0_ref, yf_ref, yb_ref, so_ref, st_ref):
    s = pl.program_id(1)

    @pl.when(_seq_start(s))
    def _():
        is_ctx = s < NB_CTX
        for gi in range(GPS):
            for r in range(N_DIR):
                s0 = jnp.concatenate([s0_ref[r, gi * S_HPG + j] for j in range(S_HPG)], axis=0)
                st_ref[gi, r] = jnp.where(is_ctx, 0.0, s0.T)

    for gi in range(GPS):
        xc = slice(gi * GW, (gi + 1) * GW)
        nc = slice(gi * S_STATE, (gi + 1) * S_STATE)
        dsk = dsk_ref[gi]
        for c in range(CPB):
            rows = slice(c * CHUNK, (c + 1) * CHUNK)
            _ssd_chunk(0, xf_ref[rows, xc], bf_ref[nc, rows], cf_ref[rows, nc], dtf_ref[gi, :, rows],
                       dsk, st_ref.at[gi], yf_ref, rows, xc)
        for c in reversed(range(CPB)):
            rows = slice(c * CHUNK, (c + 1) * CHUNK)
            _ssd_chunk(1, xb_ref[rows, xc], bb_ref[nc, rows], cb_ref[rows, nc], dtb_ref[gi, :, rows],
                       dsk, st_ref.at[gi], yb_ref, rows, xc)

    @pl.when(s < NB_CTX)
    def _():
        for gi in range(GPS):
            for r in range(N_DIR):
                stt = st_ref[gi, r].T
                for j in range(S_HPG):
                    so_ref[r, gi * S_HPG + j] = stt[j * S_HEADDIM:(j + 1) * S_HEADDIM, :]


def _ssd_scan(xs, bmt, cm, dgt, dsk, s0):
    ng = 2 * N_DIR * S_HPG
    def side(blk):
        return [
            pl.BlockSpec((TB, GPS * GW), lambda g, s: (blk(s), g)),
            pl.BlockSpec((GPS * S_STATE, TB), lambda g, s: (g, blk(s))),
            pl.BlockSpec((TB, GPS * S_STATE), lambda g, s: (blk(s), g)),
            pl.BlockSpec((GPS, ng, TB), lambda g, s: (g, 0, blk(s))),
        ]
    ident = lambda s: s
    lat = _lat_seq_of_block
    ctx = lambda s: jnp.minimum(s, NB_CTX - 1)
    st_block = (None, None, N_DIR, GPS * S_HPG, S_HEADDIM, S_STATE)
    return pl.pallas_call(
        _ssd_kernel,
        grid=(S_GROUPS // GPS, NB),
        in_specs=side(ident) + side(_bwd_block) + [
            pl.BlockSpec((GPS, 1, GW), lambda g, s: (g, 0, 0)),
            pl.BlockSpec(st_block, lambda g, s: (lat(s), 0, 0, g, 0, 0)),
        ],
        out_specs=[
            pl.BlockSpec((TB, GPS * GW), lambda g, s: (s, g)),
            pl.BlockSpec((TB, GPS * GW), lambda g, s: (_bwd_block(s), g)),
            pl.BlockSpec(st_block, lambda g, s: (ctx(s), 0, 0, g, 0, 0)),
        ],
        out_shape=[
            jax.ShapeDtypeStruct((N_TOK, S_INNER), BF16),
            jax.ShapeDtypeStruct((N_TOK, S_INNER), BF16),
            jax.ShapeDtypeStruct((BATCH, 1, N_DIR, S_HEADS, S_HEADDIM, S_STATE), F32),
        ],
        scratch_shapes=[pltpu.VMEM((GPS, N_DIR, S_STATE, GW), F32)],
        compiler_params=_cparams(("arbitrary", "arbitrary")),
        name="ssd_scan",
    )(xs, bmt, cm, dgt, xs, bmt, cm, dgt, dsk, s0)


def _l1_out_kernel(yf_ref, yb_ref, z_ref, nw_ref, w_ref, x_ref, mod_ref, lg_ref, lb_ref, wr_ref,
                   x3_ref, h4_ref, gates_ref):
    y = (yf_ref[...].astype(F32) + yb_ref[...].astype(F32)) * _silu(z_ref[...].astype(F32))
    y = y * lax.rsqrt(jnp.mean(y * y, axis=-1, keepdims=True) + RMS_EPS) * nw_ref[...]
    out = _bdot(y, w_ref[...])
    x3 = _post_ln(x_ref[...], out, mod_ref[2:3, :], lg_ref[...], lb_ref[...])
    x3_ref[...] = x3
    h4 = x3 * (1.0 + mod_ref[4:5, :]) + mod_ref[3:4, :]
    for cc in range(ROW_TILE):
        h4_ref[pl.ds(cc, TB, stride=ROW_TILE), :] = h4[:, cc * LANES:(cc + 1) * LANES]
    lane = lax.broadcasted_iota(jnp.int32, (TB, LANES), 1)
    logits = jnp.full((TB, LANES), -jnp.inf, F32)
    for e in range(N_EXPERTS):
        le = jnp.sum(h4 * wr_ref[e:e + 1, :], axis=-1, keepdims=True)
        logits = jnp.where(lane == e, le, logits)
    v1 = jnp.max(logits, axis=-1, keepdims=True)
    i1 = jnp.min(jnp.where(logits == v1, lane, LANES), axis=-1, keepdims=True)
    rest = jnp.where(lane == i1, -jnp.inf, logits)
    v2 = jnp.max(rest, axis=-1, keepdims=True)
    i2 = jnp.min(jnp.where(rest == v2, lane, LANES), axis=-1, keepdims=True)
    e2 = jnp.exp(v2 - v1)
    w1 = 1.0 / (1.0 + e2)
    w2 = e2 / (1.0 + e2)
    route = jnp.where(lane == 0, w1, jnp.where(lane == 1, w2, 0.0))
    route = jnp.where(lane == 2, i1.astype(F32), jnp.where(lane == 3, i2.astype(F32), route))
    gates_ref[...] = route


def _l1_out(yf, yb, z, norm_w, w_out, x2, mod, ln_g, ln_b, wr_t):
    full = lambda shape: pl.BlockSpec(shape, lambda i: (0,) * len(shape))
    tok = lambda w: pl.BlockSpec((TB, w), lambda i: (i, 0))
    return pl.pallas_call(
        _l1_out_kernel,
        grid=(NB,),
        in_specs=[tok(S_INNER), tok(S_INNER), tok(S_INNER), full((1, S_INNER)), full((S_INNER, D_MODEL)),
                  tok(D_MODEL), _mod_spec(1, TB), full((1, D_MODEL)), full((1, D_MODEL)),
                  full((N_EXPERTS, D_MODEL))],
        out_specs=[tok(D_MODEL), pl.BlockSpec((TB * ROW_TILE, LANES), lambda i: (i, 0)), tok(LANES)],
        out_shape=[jax.ShapeDtypeStruct((N_TOK, D_MODEL), F32),
                   jax.ShapeDtypeStruct((N_TOK * ROW_TILE, LANES), F32),
                   jax.ShapeDtypeStruct((N_TOK, LANES), F32)],
        compiler_params=_cparams(("arbitrary",)),
        name="l1_out",
    )(yf, yb, z, norm_w, w_out, x2, mod, ln_g, ln_b, wr_t)


TM_E = 1024
N_ASSIGN = TOP_K * N_TOK
P_ROWS = N_ASSIGN + N_EXPERTS * TM_E
N_PT = P_ROWS // TM_E
ROW_TILE = D_MODEL // LANES
NF_MOE = E_FF // TF_MOE


def _row(i):
    return pl.ds(pl.multiple_of(i * ROW_TILE, ROW_TILE), ROW_TILE)


def _expert_kernel(te_ref, nv_ref, src_ref, srcn_ref, asg_ref, asgp_ref, h_ref, w1_ref, w3_ref, w2_ref, yc_ref,
                   xbuf, xb_ref, acc_ref, ybuf, gsem, ssem):
    j, f = pl.program_id(0), pl.program_id(1)
    n_valid = nv_ref[0]
    valid = j < n_valid
    first_f, last_f = f == 0, f == NF_MOE - 1

    def gather_row(idx_ref, r):
        pltpu.make_async_copy(h_ref.at[_row(idx_ref[0, r])], xbuf.at[_row(r)], gsem).start()

    def scatter_row(idx_ref, r):
        pltpu.make_async_copy(ybuf.at[_row(r)], yc_ref.at[_row(idx_ref[0, r])], ssem).start()

    def looped(row_fn, idx_ref):
        def body(r, carry):
            row_fn(idx_ref, r)
            return carry
        lax.fori_loop(0, TM_E, body, 0, unroll=8)

    def wait_gather():
        pltpu.make_async_copy(h_ref.at[pl.ds(0, TM_E * ROW_TILE)], xbuf, gsem).wait()

    def wait_scatter():
        pltpu.make_async_copy(ybuf, yc_ref.at[pl.ds(0, TM_E * ROW_TILE)], ssem).wait()

    def load_rows():
        for cc in range(ROW_TILE):
            xb_ref[:, cc * LANES:(cc + 1) * LANES] = xbuf[pl.ds(cc, TM_E, stride=ROW_TILE), :].astype(BF16)
        acc_ref[...] = jnp.zeros_like(acc_ref)

    def swiglu_step():
        x = xb_ref[...]
        a = jnp.dot(x, w1_ref[...], preferred_element_type=F32)
        b = jnp.dot(x, w3_ref[...], preferred_element_type=F32)
        acc_ref[...] += _bdot(_silu(a) * b, w2_ref[...])

    @pl.when(jnp.logical_and(j == 0, first_f))
    def _():
        looped(gather_row, src_ref)

    @pl.when(jnp.logical_and(first_f, j <= n_valid))
    def _():
        wait_gather()

    @pl.when(jnp.logical_and(valid, jnp.logical_and(first_f, j == 0)))
    def _():
        load_rows()
        swiglu_step()
        for r in range(TM_E):
            gather_row(srcn_ref, r)

    @pl.when(jnp.logical_and(valid, jnp.logical_and(first_f, j > 0)))
    def _():
        load_rows()
        swiglu_step()
        for r in range(TM_E):
            gather_row(srcn_ref, r)
        for r in range(TM_E):
            scatter_row(asgp_ref, r)

    @pl.when(jnp.logical_and(valid, jnp.logical_not(first_f)))
    def _():
        swiglu_step()

    @pl.when(jnp.logical_and(valid, last_f))
    def _():
        @pl.when(j > 0)
        def _():
            wait_scatter()

        for cc in range(ROW_TILE):
            ybuf[pl.ds(cc, TM_E, stride=ROW_TILE), :] = acc_ref[:, cc * LANES:(cc + 1) * LANES]

    @pl.when(jnp.logical_and(first_f, j == n_valid))
    def _():
        looped(scatter_row, asgp_ref)
        wait_scatter()

    @pl.when(jnp.logical_and(jnp.logical_not(valid), last_f))
    def _():
        ybuf[...] = jnp.zeros_like(ybuf)
        looped(scatter_row, asg_ref)
        wait_scatter()


def _experts(tile_expert, n_valid, src, asg, h4, w1, w3, w2):
    assert NF_MOE >= 2
    fidx = lambda j, f, nv: jnp.where(j < nv[0], f, NF_MOE - 1)
    smem_tile = lambda off: pl.BlockSpec(
        (None, 1, TM_E), lambda j, f, te, nv: (jnp.clip(j + off, 0, N_PT - 1), 0, 0), memory_space=pltpu.SMEM)
    src3 = src.reshape(N_PT, 1, TM_E)
    asg3 = asg.reshape(N_PT, 1, TM_E)
    return pl.pallas_call(
        _expert_kernel,
        grid_spec=pltpu.PrefetchScalarGridSpec(
            num_scalar_prefetch=2,
            grid=(N_PT, NF_MOE),
            in_specs=[smem_tile(0), smem_tile(1), smem_tile(0), smem_tile(-1),
                      pl.BlockSpec(memory_space=pl.ANY),
                      pl.BlockSpec((None, D_MODEL, TF_MOE), lambda j, f, te, nv: (te[j], 0, fidx(j, f, nv))),
                      pl.BlockSpec((None, D_MODEL, TF_MOE), lambda j, f, te, nv: (te[j], 0, fidx(j, f, nv))),
                      pl.BlockSpec((None, TF_MOE, D_MODEL), lambda j, f, te, nv: (te[j], fidx(j, f, nv), 0))],
            out_specs=pl.BlockSpec(memory_space=pl.ANY),
            scratch_shapes=[pltpu.VMEM((TM_E * ROW_TILE, LANES), F32),
                            pltpu.VMEM((TM_E, D_MODEL), BF16),
                            pltpu.VMEM((TM_E, D_MODEL), F32),
                            pltpu.VMEM((TM_E * ROW_TILE, LANES), F32),
                            pltpu.SemaphoreType.DMA(()),
                            pltpu.SemaphoreType.DMA(())],
        ),
        out_shape=jax.ShapeDtypeStruct((P_ROWS * ROW_TILE, LANES), F32),
        compiler_params=_cparams(("arbitrary", "arbitrary")),
        name="experts",
    )(tile_expert, n_valid, src3, src3, asg3, asg3, h4, w1, w3, w2)


def _moe_out_kernel(yc_ref, r_ref, x_ref, mod_ref, lg_ref, lb_ref, yp_ref, ys_ref, f_ref):
    i = pl.program_id(0)
    w1, w2 = r_ref[:, 0:1], r_ref[:, 1:2]
    for cc in range(ROW_TILE):
        y_top1 = yc_ref[pl.ds(cc, TM_FF, stride=TOP_K * ROW_TILE), :]
        y_top2 = yc_ref[pl.ds(ROW_TILE + cc, TM_FF, stride=TOP_K * ROW_TILE), :]
        f_ref[:, cc * LANES:(cc + 1) * LANES] = w1 * y_top1 + w2 * y_top2
    y = _post_ln(x_ref[...], f_ref[...], mod_ref[5:6, :], lg_ref[...], lb_ref[...])

    @pl.when(i < N_CTX // TM_FF)
    def _():
        yp_ref[...] = y

    @pl.when(i >= N_CTX // TM_FF)
    def _():
        ys_ref[...] = y


def _moe_out(yc, route, x3, mod, ln_g, ln_b):
    nctx = N_CTX // TM_FF
    tok = lambda w: pl.BlockSpec((TM_FF, w), lambda i: (i, 0))
    vec = pl.BlockSpec((1, D_MODEL), lambda i: (0, 0))
    return pl.pallas_call(
        _moe_out_kernel,
        grid=(N_TOK // TM_FF,),
        in_specs=[pl.BlockSpec((TM_FF * TOP_K * ROW_TILE, LANES), lambda i: (i, 0)),
                  tok(LANES), tok(D_MODEL), _mod_spec(1, TM_FF), vec, vec],
        out_specs=[pl.BlockSpec((TM_FF, D_MODEL), lambda i: (jnp.minimum(i, nctx - 1), 0)),
                   pl.BlockSpec((TM_FF, D_MODEL), lambda i: (jnp.maximum(i - nctx, 0), 0))],
        out_shape=[jax.ShapeDtypeStruct((N_CTX, D_MODEL), F32),
                   jax.ShapeDtypeStruct((N_LAT, D_MODEL), F32)],
        scratch_shapes=[pltpu.VMEM((TM_FF, D_MODEL), F32)],
        compiler_params=_cparams(("arbitrary",)),
        name="moe_out",
    )(yc, route, x3, mod, ln_g, ln_b)


def _routing_tables(route):
    flat_e = route[:, 2:2 + TOP_K].astype(jnp.int32).reshape(N_ASSIGN)
    onehot = (flat_e[:, None] == jnp.arange(N_EXPERTS, dtype=jnp.int32)[None, :]).astype(jnp.int32)
    csum = jnp.cumsum(onehot, axis=0)
    rank = jnp.take_along_axis(csum, flat_e[:, None], axis=1)[:, 0] - 1
    ntile = (csum[-1] + TM_E - 1) // TM_E
    tile_end = jnp.cumsum(ntile)
    dest = (tile_end - ntile)[flat_e] * TM_E + rank
    asg = jnp.full((P_ROWS,), -1, jnp.int32).at[dest].set(jnp.arange(N_ASSIGN, dtype=jnp.int32))
    is_pad = asg < 0
    src = jnp.maximum(asg, 0) // TOP_K
    asg = jnp.where(is_pad, N_ASSIGN - 1 + jnp.cumsum(is_pad.astype(jnp.int32)), asg)
    tiles = jnp.arange(N_PT, dtype=jnp.int32)
    n_valid = tile_end[-1]
    tile_expert = jnp.sum((jnp.minimum(tiles, n_valid - 1)[:, None] >= tile_end[None, :]).astype(jnp.int32), axis=1)
    return src, asg, tile_expert, n_valid.reshape(1)


def _grid_pos_embed():
    rows = DEC_SEQ // GRID_W
    quarter = D_MODEL // 4
    freqs = jnp.exp(-math.log(POS_BASE) * jnp.arange(quarter, dtype=F32) / quarter)
    er = jnp.arange(rows, dtype=F32)[:, None] * freqs
    ec = jnp.arange(GRID_W, dtype=F32)[:, None] * freqs
    row_emb = jnp.concatenate([jnp.sin(er), jnp.cos(er)], axis=-1)
    col_emb = jnp.concatenate([jnp.sin(ec), jnp.cos(ec)], axis=-1)
    pos = jnp.concatenate([jnp.broadcast_to(row_emb[:, None, :], (rows, GRID_W, D_MODEL // 2)),
                           jnp.broadcast_to(col_emb[None, :, :], (rows, GRID_W, D_MODEL // 2))], axis=-1)
    return pos.reshape(DEC_SEQ, D_MODEL)


def _pad_lanes(a, width=LANES):
    return jnp.pad(a, [(0, 0)] * (a.ndim - 1) + [(0, width - a.shape[-1])])


def kernel(x_prompt, x_sample, c, state_mlstm_C, state_mlstm_n, state_mlstm_m, state_ssm, c_ctx,
           ada_w, ada_b, ln_g, ln_b,
           ml_w_up, ml_conv_w, ml_conv_b, ml_wq, ml_wk, ml_wv, ml_w_ig, ml_b_ig, ml_w_fg, ml_b_fg,
           ml_norm_w, ml_skip, ml_w_down,
           ss_w_in, ss_conv_w, ss_conv_b, ss_dt_bias, ss_a_log, ss_d, ss_norm_w, ss_w_out,
           ff_w1, ff_w3, ff_w2,
           moe_w_router, moe_w1, moe_w3, moe_w2):
    row = lambda a: a.reshape(1, -1)
    cond = jnp.concatenate([c_ctx[None, :], c, jnp.zeros((COND_ROWS - N_COND, D_MODEL), F32)], axis=0)
    mod = _ada_mod(cond, ada_w.astype(BF16), ada_b)

    xa, xm, z0 = _l0_in(x_prompt.reshape(N_CTX, D_MODEL), x_sample.reshape(N_LAT, D_MODEL), _grid_pos_embed(),
                        mod, ml_w_up[0].astype(BF16))
    ng = N_DIR * M_HEADS
    wg = jnp.concatenate([ml_w_ig[0].transpose(1, 0, 2).reshape(3 * M_INNER, ng),
                          ml_w_fg[0].transpose(1, 0, 2).reshape(3 * M_INNER, ng)], axis=1)
    bg = jnp.concatenate([ml_b_ig[0].reshape(ng), ml_b_fg[0].reshape(ng)])
    xc, q, k, v, g, bc = _l0_qkv(xm, ml_conv_w[0], row(ml_conv_b[0]),
                                 ml_wq[0].astype(BF16), ml_wk[0].astype(BF16), ml_wv[0].astype(BF16),
                                 _pad_lanes(wg).astype(BF16), row(_pad_lanes(bg)))
    gh = jnp.concatenate([g[:, :ng].reshape(N_TOK, N_DIR, M_HEADS),
                          bc[:, ng:2 * ng].reshape(N_TOK, N_DIR, M_HEADS)], axis=1)
    ght = gh.transpose(2, 1, 0)
    gh = gh.transpose(2, 0, 1)
    hf, hb, new_c, new_n, new_m = _mlstm_scan(
        q, k, v, gh, ght, state_mlstm_C,
        state_mlstm_n.reshape(DEC_BATCH, N_DIR, M_HEADS, 1, M_DH),
        state_mlstm_m.reshape(DEC_BATCH, N_DIR, M_HEADS, 1, 1))
    x1, h2 = _l0_out(hf, hb, xc, z0, row(ml_norm_w[0]), row(ml_skip[0]), ml_w_down[0].astype(BF16),
                     xa, mod, row(ln_g[0, 0]), row(ln_b[0, 0]))
    x2, h3 = _ffn(h2, ff_w1[0].astype(BF16), ff_w3[0].astype(BF16), ff_w2[0].astype(BF16),
                  x1, mod, row(ln_g[0, 1]), row(ln_b[0, 1]))

    w_in = ss_w_in[0]
    z1, xbc, dt_raw = _l1_in(h3, w_in[:, :S_INNER].astype(BF16),
                             w_in[:, S_INNER:S_INNER + S_CONV_CH].astype(BF16),
                             _pad_lanes(w_in[:, S_INNER + S_CONV_CH:]).astype(BF16))
    a_neg = -jnp.exp(ss_a_log[0].reshape(N_DIR * S_HEADS))
    xs, bmt, cm, dt, cum = _l1_conv(xbc, ss_conv_w[0], row(ss_conv_b[0]), dt_raw,
                                    row(_pad_lanes(ss_dt_bias[0].reshape(N_DIR * S_HEADS))), row(_pad_lanes(a_neg)))
    nh = N_DIR * S_HEADS
    to_group = lambda a: a[:, :nh].reshape(N_TOK, N_DIR, S_GROUPS, S_HPG).transpose(2, 0, 1, 3).reshape(
        S_GROUPS, N_TOK, N_DIR * S_HPG)
    dgt = jnp.concatenate([to_group(dt), to_group(cum)], axis=-1).transpose(0, 2, 1)
    dsk = jnp.repeat(ss_d[0], S_HEADDIM).reshape(S_GROUPS, 1, GW)
    yf, yb, new_s = _ssd_scan(xs, bmt, cm, dgt, dsk, state_ssm)
    x3, h4, route = _l1_out(yf, yb, z1, row(ss_norm_w[0]), ss_w_out[0].astype(BF16), x2, mod,
                            row(ln_g[1, 0]), row(ln_b[1, 0]), moe_w_router[0].T)
    src, asg, tile_expert, n_valid = _routing_tables(route)
    yc = _experts(tile_expert, n_valid, src, asg, h4,
                  moe_w1[0].astype(BF16), moe_w3[0].astype(BF16), moe_w2[0].astype(BF16))
    yp, ys = _moe_out(yc, route, x3, mod, row(ln_g[1, 1]), row(ln_b[1, 1]))

    return (yp.reshape(BATCH, SEQ, D_MODEL), ys.reshape(DEC_BATCH, DEC_SEQ, D_MODEL),
            new_c,
            new_n.reshape(BATCH, 1, N_DIR, M_HEADS, M_DH),
            new_m.reshape(BATCH, 1, N_DIR, M_HEADS),
            new_s)
```
